```python
import math
import jax, jax.numpy as jnp
from jax import lax
import numpy as np

D_MODEL = 2048
BATCH = 4
SEQ = 2048
DEPTH = 2
DEC_BATCH = 128
DEC_SEQ = 4
PAST_LEN = 16384
PAGE_SIZE = 128

HA_DK = 128
HA_WIDTH = D_MODEL // 2
HA_HEADS = HA_WIDTH // HA_DK
HA_DV = HA_WIDTH // HA_HEADS
HB_WIDTH = D_MODEL // 4
HB_BLOCKS = 8
HB_BW = HB_WIDTH // HB_BLOCKS
CONV_W = 4
LRU_C = 8.0
HC_HEADS = 4
HC_WIDTH = D_MODEL - HA_WIDTH - HB_WIDTH
HC_DH = HC_WIDTH // HC_HEADS
N_MEM = 256
IN_COLS = 4 * HA_WIDTH + 2 * HB_WIDTH + HC_WIDTH
SPLITS = (HA_WIDTH, 2 * HA_WIDTH, 3 * HA_WIDTH, 4 * HA_WIDTH,
          4 * HA_WIDTH + HB_WIDTH, 4 * HA_WIDTH + 2 * HB_WIDTH)
HGRN_CHUNK = 64
N_EXPERTS = 32
TOP_K = 4
D_FF = D_MODEL
SWIGLU_LIMIT = 7.0
SWIGLU_ALPHA = 1.702
EPS = 1e-6

kernel_name = "hymba_hgrn2_rglru_memxattn_moe_step"


def rms_norm(x, g):
    xf = x.astype(jnp.float32)
    y = xf * lax.rsqrt(jnp.mean(xf * xf, axis=-1, keepdims=True) + EPS)
    return (y * g.astype(jnp.float32)).astype(x.dtype)


def hgrn_lower_bounds(lb_param):
    p = jax.nn.softmax(lb_param.astype(jnp.float32), axis=0)
    c = jnp.cumsum(p, axis=0)
    return c - c[0:1]


def chunked_gated_recurrence(q, k, v, logf, S0):
    B, T, H, dk = q.shape
    dv = v.shape[-1]
    C = HGRN_CHUNK if T % HGRN_CHUNK == 0 else T
    n = T // C

    def to_chunks(a):
        return jnp.moveaxis(a.reshape(B, n, C, H, a.shape[-1]), 1, 0)

    mask = jnp.tril(jnp.ones((C, C), dtype=bool))[None, :, :, None, None]

    def step(S, inp):
        qc, kc, vc, gc = inp
        b = jnp.cumsum(gc, axis=1)
        o_inter = jnp.einsum('bchk,bhkv->bchv', qc * jnp.exp(b), S)
        diff = b[:, :, None] - b[:, None]
        decay = jnp.exp(jnp.where(mask, diff, -jnp.inf))
        att = jnp.einsum('bthk,btshk,bshk->bhts', qc, decay, kc)
        o_intra = jnp.einsum('bhts,bshv->bthv', att, vc)
        bC = b[:, -1]
        S_new = jnp.exp(bC)[..., None] * S + jnp.einsum(
            'bshk,bshv->bhkv', kc * jnp.exp(bC[:, None] - b), vc)
        return S_new, o_inter + o_intra

    S_fin, o = lax.scan(step, S0, (to_chunks(q), to_chunks(k), to_chunks(v), to_chunks(logf)))
    o = jnp.moveaxis(o, 0, 1).reshape(B, T, H, dv)
    return o, S_fin


def hgrn2_mixer(qa, fa, ia, ga, lb, onorm_g, S0):
    B, T, _ = qa.shape
    shp = (B, T, HA_HEADS, HA_DK)
    z = fa.reshape(shp).astype(jnp.float32)
    lbh = lb.reshape(HA_HEADS, HA_DK)
    f = lbh + (1.0 - lbh) * jax.nn.sigmoid(z)
    logf = jnp.log(f)
    k = (1.0 - lbh) * jax.nn.sigmoid(-z)
    q = jax.nn.silu(qa.reshape(shp).astype(jnp.float32))
    v = ia.reshape(B, T, HA_HEADS, HA_DV).astype(jnp.float32)
    o, S = chunked_gated_recurrence(q, k, v, logf, S0.astype(jnp.float32))
    o = rms_norm(o, onorm_g) * jax.nn.silu(ga.reshape(B, T, HA_HEADS, HA_DV).astype(jnp.float32))
    return o.reshape(B, T, HA_WIDTH).astype(qa.dtype), S


def lru_scan(a, b, h0):
    b = b.at[:, 0].add(a[:, 0] * h0)

    def comb(l, r):
        return (l[0] * r[0], r[0] * l[1] + r[1])

    _, h = lax.associative_scan(comb, (a, b), axis=1)
    return h


def rglru_mixer(xb, gb, conv_buf, h0, conv_w, conv_b, wa, ba, wx, bx, L):
    B, T, _ = xb.shape
    xp = jnp.concatenate([conv_buf.astype(xb.dtype), xb], axis=1)
    new_buf = xp[:, -(CONV_W - 1):]
    xc = conv_b + sum(conv_w[j] * xp[:, j:j + T] for j in range(CONV_W))
    xh = xc.reshape(B, T, HB_BLOCKS, HB_BW)
    r = jax.nn.sigmoid((jnp.einsum('bthi,hij->bthj', xh, wa).reshape(B, T, HB_WIDTH) + ba).astype(jnp.float32))
    i = jax.nn.sigmoid((jnp.einsum('bthi,hij->bthj', xh, wx).reshape(B, T, HB_WIDTH) + bx).astype(jnp.float32))
    log_a = -LRU_C * r * jax.nn.softplus(-L.astype(jnp.float32))
    a = jnp.exp(log_a)
    mult = jnp.sqrt(-jnp.expm1(2.0 * log_a))
    h = lru_scan(a, mult * i * xc.astype(jnp.float32), h0.astype(jnp.float32))
    y = h * jax.nn.gelu(gb.astype(jnp.float32))
    return y.astype(xb.dtype), h[:, -1], new_buf


def memory_xattn(qc, mem_k, mem_v):
    B, T, _ = qc.shape
    q = qc.reshape(B, T, HC_HEADS, HC_DH)
    s = jnp.einsum('bthd,bmhd->bhtm', q, mem_k).astype(jnp.float32) * (1.0 / math.sqrt(HC_DH))
    p = jax.nn.softmax(s, axis=-1).astype(mem_v.dtype)
    o = jnp.einsum('bhtm,bmhd->bthd', p, mem_v)
    return o.reshape(B, T, HC_WIDTH)


def moe(h, w_r, b_r, w1, b1, w2, b2):
    logits = (h @ w_r + b_r).astype(jnp.float32)
    topv, topi = lax.top_k(logits, TOP_K)
    gates = jax.nn.softmax(topv, axis=-1)
    flat_e = topi.reshape(-1)
    order = jnp.argsort(flat_e)
    tok = order // TOP_K
    e_sorted = flat_e[order]
    group_sizes = jnp.bincount(flat_e, length=N_EXPERTS).astype(jnp.int32)
    xs = h[tok]
    u = lax.ragged_dot(xs, w1, group_sizes) + b1[e_sorted]
    u = u.astype(jnp.float32)
    glu, lin = u[:, :D_FF], u[:, D_FF:]
    glu = jnp.minimum(glu, SWIGLU_LIMIT)
    lin = jnp.clip(lin, -SWIGLU_LIMIT, SWIGLU_LIMIT)
    act = (glu * jax.nn.sigmoid(SWIGLU_ALPHA * glu) * (lin + 1.0)).astype(h.dtype)
    out = lax.ragged_dot(act, w2, group_sizes) + b2[e_sorted]
    out = out * gates.reshape(-1)[order][:, None].astype(out.dtype)
    return jnp.zeros_like(h).at[tok].add(out.astype(h.dtype))


def trunk(x, mem_k, mem_v, s_hgrn, s_lru, s_conv, P):
    lb_all = hgrn_lower_bounds(P['hgrn_lb'])
    new_h, new_l, new_c = [], [], []
    for l in range(DEPTH):
        xn = rms_norm(x, P['norm_mix'][l])
        u = xn @ P['w_in'][l]
        qa, fa, ia, ga, xb, gb, qc = jnp.split(u, SPLITS, axis=-1)
        oa, S = hgrn2_mixer(qa, fa, ia, ga, lb_all[l], P['hgrn_onorm'][l], s_hgrn[l])
        ob, hl, cb = rglru_mixer(xb, gb, s_conv[l], s_lru[l], P['conv_w'][l], P['conv_b'][l],
                                 P['lru_wa'][l], P['lru_ba'][l], P['lru_wx'][l], P['lru_bx'][l], P['lru_L'][l])
        oc = memory_xattn(qc, mem_k[l], mem_v[l])
        mix = jnp.concatenate([oa, rms_norm(ob, P['norm_lru'][l]), rms_norm(oc, P['norm_xattn'][l])], axis=-1)
        x = x + mix @ P['w_out'][l]
        hn = rms_norm(x, P['norm_ffn'][l])
        y = moe(hn.reshape(-1, D_MODEL), P['router_w'][l], P['router_b'][l],
                P['moe_w1'][l], P['moe_b1'][l], P['moe_w2'][l], P['moe_b2'][l])
        x = x + y.reshape(x.shape)
        new_h.append(S.astype(x.dtype))
        new_l.append(hl.astype(x.dtype))
        new_c.append(cb.astype(x.dtype))
    return rms_norm(x, P['norm_final']), jnp.stack(new_h), jnp.stack(new_l), jnp.stack(new_c)


def setup_inputs(seed: int = 0) -> dict:
    key = jax.random.key(seed)
    ks = iter(jax.random.split(key, 48))
    f32 = jnp.float32

    def nrm(shape, scale):
        return jax.random.normal(next(ks), shape, f32) * scale

    def gain(shape):
        return 1.0 + nrm(shape, 0.05)

    u = jax.random.uniform(next(ks), (DEPTH, HB_WIDTH), f32, minval=0.9, maxval=0.999)
    s = u ** (1.0 / LRU_C)
    lru_L = jnp.log(s) - jnp.log1p(-s)
    return {
        'x_prompt': nrm((BATCH, SEQ, D_MODEL), 1.0),
        'x_sample': nrm((DEC_BATCH, DEC_SEQ, D_MODEL), 1.0),
        'mem_prompt': nrm((BATCH, N_MEM, D_MODEL), 1.0),
        'state_hgrn': nrm((DEPTH, DEC_BATCH, HA_HEADS, HA_DK, HA_DV), 0.5),
        'state_lru': nrm((DEPTH, DEC_BATCH, HB_WIDTH), 0.5),
        'state_conv': nrm((DEPTH, DEC_BATCH, CONV_W - 1, HB_WIDTH), 1.0),
        'cache_mem_k': nrm((DEPTH, DEC_BATCH, N_MEM, HC_HEADS, HC_DH), 1.0),
        'cache_mem_v': nrm((DEPTH, DEC_BATCH, N_MEM, HC_HEADS, HC_DH), 1.0),
        'norm_mix': gain((DEPTH, D_MODEL)),
        'w_in': nrm((DEPTH, D_MODEL, IN_COLS), D_MODEL ** -0.5),
        'hgrn_lb': nrm((DEPTH, HA_WIDTH), 0.1),
        'hgrn_onorm': gain((DEPTH, HA_DV)),
        'conv_w': nrm((DEPTH, CONV_W, HB_WIDTH), CONV_W ** -0.5),
        'conv_b': nrm((DEPTH, HB_WIDTH), 0.02),
        'lru_wa': nrm((DEPTH, HB_BLOCKS, HB_BW, HB_BW), HB_BW ** -0.5),
        'lru_ba': nrm((DEPTH, HB_WIDTH), 0.1),
        'lru_wx': nrm((DEPTH, HB_BLOCKS, HB_BW, HB_BW), HB_BW ** -0.5),
        'lru_bx': nrm((DEPTH, HB_WIDTH), 0.1),
        'lru_L': lru_L,
        'norm_lru': gain((DEPTH, HB_WIDTH)),
        'norm_xattn': gain((DEPTH, HC_WIDTH)),
        'norm_mem': gain((DEPTH, D_MODEL)),
        'w_mem_k': nrm((DEPTH, D_MODEL, HC_WIDTH), D_MODEL ** -0.5),
        'w_mem_v': nrm((DEPTH, D_MODEL, HC_WIDTH), D_MODEL ** -0.5),
        'w_out': nrm((DEPTH, D_MODEL, D_MODEL), D_MODEL ** -0.5),
        'norm_ffn': gain((DEPTH, D_MODEL)),
        'router_w': nrm((DEPTH, D_MODEL, N_EXPERTS), D_MODEL ** -0.5),
        'router_b': nrm((DEPTH, N_EXPERTS), 0.01),
        'moe_w1': nrm((DEPTH, N_EXPERTS, D_MODEL, 2 * D_FF), D_MODEL ** -0.5),
        'moe_b1': nrm((DEPTH, N_EXPERTS, 2 * D_FF), 0.02),
        'moe_w2': nrm((DEPTH, N_EXPERTS, D_FF, D_MODEL), 0.5 * D_FF ** -0.5),
        'moe_b2': nrm((DEPTH, N_EXPERTS, D_MODEL), 0.02),
        'norm_final': gain((D_MODEL,)),
    }


def reference(x_prompt, x_sample, mem_prompt, state_hgrn, state_lru, state_conv, cache_mem_k, cache_mem_v,
              norm_mix, w_in, hgrn_lb, hgrn_onorm, conv_w, conv_b, lru_wa, lru_ba, lru_wx, lru_bx, lru_L,
              norm_lru, norm_xattn, norm_mem, w_mem_k, w_mem_v, w_out, norm_ffn, router_w, router_b,
              moe_w1, moe_b1, moe_w2, moe_b2, norm_final):
    P = dict(norm_mix=norm_mix, w_in=w_in, hgrn_lb=hgrn_lb, hgrn_onorm=hgrn_onorm, conv_w=conv_w,
             conv_b=conv_b, lru_wa=lru_wa, lru_ba=lru_ba, lru_wx=lru_wx, lru_bx=lru_bx, lru_L=lru_L,
             norm_lru=norm_lru, norm_xattn=norm_xattn, w_out=w_out, norm_ffn=norm_ffn,
             router_w=router_w, router_b=router_b, moe_w1=moe_w1, moe_b1=moe_b1, moe_w2=moe_w2,
             moe_b2=moe_b2, norm_final=norm_final)
    dt = x_prompt.dtype
    mks, mvs = [], []
    for l in range(DEPTH):
        mn = rms_norm(mem_prompt, norm_mem[l])
        mks.append((mn @ w_mem_k[l]).reshape(BATCH, N_MEM, HC_HEADS, HC_DH))
        mvs.append((mn @ w_mem_v[l]).reshape(BATCH, N_MEM, HC_HEADS, HC_DH))
    p_mem_k = jnp.stack(mks)
    p_mem_v = jnp.stack(mvs)
    z_hgrn = jnp.zeros((DEPTH, BATCH, HA_HEADS, HA_DK, HA_DV), dt)
    z_lru = jnp.zeros((DEPTH, BATCH, HB_WIDTH), dt)
    z_conv = jnp.zeros((DEPTH, BATCH, CONV_W - 1, HB_WIDTH), dt)
    y_prompt, p_hgrn, p_lru, p_conv = trunk(x_prompt, p_mem_k, p_mem_v, z_hgrn, z_lru, z_conv, P)
    y_sample, s_hgrn, s_lru, s_conv = trunk(x_sample, cache_mem_k, cache_mem_v,
                                            state_hgrn, state_lru, state_conv, P)
    return (y_prompt, y_sample, p_hgrn, p_lru, p_conv, p_mem_k, p_mem_v, s_hgrn, s_lru, s_conv)
```

```python
import functools
import math

import jax
import jax.numpy as jnp
from jax import lax
from jax.experimental import pallas as pl
from jax.experimental.pallas import tpu as pltpu

F32 = jnp.float32
BF16 = jnp.bfloat16

D_MODEL = 2048
BATCH = 4
SEQ = 2048
DEPTH = 2
DEC_BATCH = 128
DEC_SEQ = 4
HA_DK = 128
HA_WIDTH = D_MODEL // 2
HA_HEADS = HA_WIDTH // HA_DK
HA_DV = HA_WIDTH // HA_HEADS
HB_WIDTH = D_MODEL // 4
HB_BLOCKS = 8
HB_BW = HB_WIDTH // HB_BLOCKS
CONV_W = 4
LRU_C = 8.0
HC_HEADS = 4
HC_WIDTH = D_MODEL - HA_WIDTH - HB_WIDTH
HC_DH = HC_WIDTH // HC_HEADS
N_MEM = 256
IN_COLS = 4 * HA_WIDTH + 2 * HB_WIDTH + HC_WIDTH
N_EXPERTS = 32
TOP_K = 4
D_FF = D_MODEL
SWIGLU_LIMIT = 7.0
SWIGLU_ALPHA = 1.702
EPS = 1e-6

N_P = BATCH * SEQ
N_S = DEC_BATCH * DEC_SEQ
N_ALL = N_P + N_S

COL_QA, COL_FA, COL_IA, COL_GA = 0, 1, 2, 3
COL_XB, COL_GB, COL_QC = 8, 9, 10

HGRN_CHUNK = 64
HGRN_SUB = 16
LRU_TB = 256
XATTN_TQ = 512
MOE_TM = 512
MOE_TN1 = 512
MOE_TN2 = 512
MOE_TILES = (N_ALL * TOP_K) // MOE_TM + N_EXPERTS
MOE_ROWS = MOE_TILES * MOE_TM
VMEM_LIMIT = 56 * 1024 * 1024

NT_DIMS = (((1,), (1,)), ((), ()))
TN_DIMS = (((0,), (0,)), ((), ()))


def _cparams(sem):
    return pltpu.CompilerParams(dimension_semantics=sem, vmem_limit_bytes=VMEM_LIMIT)


def _rms(x, g):
    return x * lax.rsqrt(jnp.mean(x * x, axis=-1, keepdims=True) + EPS) * g


def _sigmoid(x):
    return jax.nn.sigmoid(x)


def _silu(x):
    return x * jax.nn.sigmoid(x)


def _norm_matmul_body(x_ref, g_ref, w_ref, o_ref, xn_ref):
    @pl.when(pl.program_id(1) == 0)
    def _():
        xn_ref[...] = _rms(x_ref[...], g_ref[...]).astype(BF16)

    o_ref[...] = jnp.dot(xn_ref[...], w_ref[...], preferred_element_type=F32)


def norm_matmul(x, g, w, tm, tn):
    n, d = x.shape
    nc = w.shape[1]
    return pl.pallas_call(
        _norm_matmul_body,
        grid=(n // tm, nc // tn),
        in_specs=[
            pl.BlockSpec((tm, d), lambda i, j: (i, 0)),
            pl.BlockSpec((1, d), lambda i, j: (0, 0)),
            pl.BlockSpec((d, tn), lambda i, j: (0, j)),
        ],
        out_specs=pl.BlockSpec((tm, tn), lambda i, j: (i, j)),
        out_shape=jax.ShapeDtypeStruct((n, nc), F32),
        scratch_shapes=[pltpu.VMEM((tm, d), BF16)],
        compiler_params=_cparams(("parallel", "arbitrary")),
        name="norm_matmul",
    )(x, g.reshape(1, d), w)


def _hgrn_lower_bound(lbp, layer):
    rows = [lbp[r:r + 1, :] for r in range(DEPTH)]
    m = functools.reduce(jnp.maximum, rows)
    es = [jnp.exp(r - m) for r in rows]
    tot = functools.reduce(lambda a, b: a + b, es)
    lb = jnp.zeros_like(m)
    for r in range(1, layer + 1):
        lb = lb + es[r] / tot
    return lb


def _hgrn_gates(z, qa, lb):
    f = lb + (1.0 - lb) * _sigmoid(z)
    k = (1.0 - lb) * _sigmoid(-z)
    q = _silu(qa)
    return f, k, q


def _hgrn_prompt_body(layer, qa_ref, fa_ref, ia_ref, ga_ref, lbp_ref, on_ref, oa_ref, st_ref, s_scr):
    c = pl.program_id(1)
    C, SUB = HGRN_CHUNK, HGRN_SUB
    nsub = C // SUB

    @pl.when(c == 0)
    def _():
        s_scr[...] = jnp.zeros_like(s_scr)

    lb = _hgrn_lower_bound(lbp_ref[...], layer)
    f, k, q = _hgrn_gates(fa_ref[...], qa_ref[...], lb)
    g = jnp.log(f)
    v = ia_ref[...]
    gate = _silu(ga_ref[...])

    row = lax.broadcasted_iota(jnp.int32, (C, C), 0)
    col = lax.broadcasted_iota(jnp.int32, (C, C), 1)
    tri = (row >= col).astype(F32)
    b_all = jnp.dot(tri, g, preferred_element_type=F32, precision=lax.Precision.HIGHEST)

    row_sub = lax.broadcasted_iota(jnp.int32, (SUB, HA_DK), 0)
    row_c = lax.broadcasted_iota(jnp.int32, (C, HA_DK), 0)
    lane_c = lax.broadcasted_iota(jnp.int32, (SUB, C), 1)
    neg_inf = jnp.float32(-jnp.inf)

    for h in range(HA_HEADS):
        sl = slice(h * HA_DK, (h + 1) * HA_DK)
        bh, qh, kh, vh = b_all[:, sl], q[:, sl], k[:, sl], v[:, sl]
        kh16 = kh.astype(BF16)
        vh16 = vh.astype(BF16)
        b_end = bh[C - 1:C, :]
        st = s_scr[h]
        o = lax.dot_general((qh * jnp.exp(bh)).astype(BF16), st.astype(BF16), NT_DIMS,
                            preferred_element_type=F32)
        kdec = (kh * jnp.exp(b_end - bh)).astype(BF16)
        upd = lax.dot_general(vh16, kdec, TN_DIMS, preferred_element_type=F32)
        s_new = st * jnp.exp(b_end) + upd
        s_scr[h] = s_new

        a_rows = []
        for i in range(nsub):
            bi = bh[i * SUB:(i + 1) * SUB, :]
            qi = qh[i * SUB:(i + 1) * SUB, :]
            parts = []
            for s in range(SUB):
                dec = jnp.exp(jnp.where(row_sub >= s, bi - bi[s:s + 1, :], neg_inf))
                parts.append(qi * dec)
            q_all = jnp.concatenate(parts, axis=0).astype(BF16)
            m = lax.dot_general(q_all, kh16, NT_DIMS, preferred_element_type=F32)
            a_i = jnp.zeros((SUB, C), F32)
            for s in range(SUB):
                a_i = a_i + jnp.where(lane_c == i * SUB + s, m[s * SUB:(s + 1) * SUB, :], 0.0)
            if i > 0:
                b_prev = bh[i * SUB - 1:i * SUB, :]
                q_i = (qi * jnp.exp(bi - b_prev)).astype(BF16)
                k_i = (kh * jnp.exp(jnp.where(row_c < i * SUB, b_prev - bh, neg_inf))).astype(BF16)
                a_i = a_i + lax.dot_general(q_i, k_i, NT_DIMS, preferred_element_type=F32)
            a_rows.append(a_i)
        att = jnp.concatenate(a_rows, axis=0).astype(BF16)
        o = o + jnp.dot(att, vh16, preferred_element_type=F32)
        oa_ref[:, sl] = _rms(o, on_ref[...]) * gate[:, sl]

        @pl.when(c == pl.num_programs(1) - 1)
        def _():
            st_ref[0, h] = s_new.T


def hgrn_prompt(layer, u, lbp, onorm):
    nchunk = SEQ // HGRN_CHUNK

    def spec(colblk):
        return pl.BlockSpec((HGRN_CHUNK, HA_WIDTH), lambda b, c: (b * nchunk + c, colblk))

    return pl.pallas_call(
        functools.partial(_hgrn_prompt_body, layer),
        grid=(BATCH, nchunk),
        in_specs=[spec(COL_QA), spec(COL_FA), spec(COL_IA), spec(COL_GA),
                  pl.BlockSpec((DEPTH, HA_WIDTH), lambda b, c: (0, 0)),
                  pl.BlockSpec((1, HA_DV), lambda b, c: (0, 0))],
        out_specs=[pl.BlockSpec((HGRN_CHUNK, HA_WIDTH), lambda b, c: (b * nchunk + c, 0)),
                   pl.BlockSpec((1, HA_HEADS, HA_DK, HA_DV), lambda b, c: (b, 0, 0, 0))],
        out_shape=[jax.ShapeDtypeStruct((N_ALL, HA_WIDTH), F32),
                   jax.ShapeDtypeStruct((BATCH, HA_HEADS, HA_DK, HA_DV), F32)],
        scratch_shapes=[pltpu.VMEM((HA_HEADS, HA_DV, HA_DK), F32)],
        compiler_params=_cparams(("parallel", "arbitrary")),
        name="hgrn_prompt",
    )(u, u, u, u, lbp, onorm.reshape(1, HA_DV))


HGRN_SB = 2


def _hgrn_sample_body(layer, qa_ref, fa_ref, ia_ref, ga_ref, lbp_ref, on_ref, s_ref, oa_in_ref, so_in_ref,
                      oa_ref, so_ref):
    del oa_in_ref, so_in_ref
    lb = _hgrn_lower_bound(lbp_ref[...], layer)
    f, k, q = _hgrn_gates(fa_ref[...], qa_ref[...], lb)
    v = ia_ref[...]
    gate = _silu(ga_ref[...])
    rows8 = HGRN_SB * DEC_SEQ
    for h in range(HA_HEADS):
        sl = slice(h * HA_DK, (h + 1) * HA_DK)
        ft, kt, qt = f[:, sl].T, k[:, sl].T, q[:, sl].T
        outs = []
        for bb in range(HGRN_SB):
            s = s_ref[0, bb, h]
            for t in range(DEC_SEQ):
                j = bb * DEC_SEQ + t
                s = ft[:, j:j + 1] * s + kt[:, j:j + 1] * v[j:j + 1, sl]
                outs.append(jnp.sum(qt[:, j:j + 1] * s, axis=0, keepdims=True))
            so_ref[0, bb, h] = s
        o = jnp.concatenate(outs, axis=0)
        oa_ref[:, sl] = _rms(o, on_ref[...]) * gate[:, sl]


def hgrn_sample(layer, u, lbp, onorm, state, oa_buf, so_buf):
    rows8 = HGRN_SB * DEC_SEQ
    row0 = N_P // rows8

    def spec(colblk):
        return pl.BlockSpec((rows8, HA_WIDTH), lambda i: (row0 + i, colblk))

    st_spec = pl.BlockSpec((1, HGRN_SB, HA_HEADS, HA_DK, HA_DV), lambda i: (layer, i, 0, 0, 0))
    any_spec = pl.BlockSpec(memory_space=pl.ANY)
    return pl.pallas_call(
        functools.partial(_hgrn_sample_body, layer),
        grid=(DEC_BATCH // HGRN_SB,),
        in_specs=[spec(COL_QA), spec(COL_FA), spec(COL_IA), spec(COL_GA),
                  pl.BlockSpec((DEPTH, HA_WIDTH), lambda i: (0, 0)),
                  pl.BlockSpec((1, HA_DV), lambda i: (0, 0)),
                  st_spec, any_spec, any_spec],
        out_specs=[pl.BlockSpec((rows8, HA_WIDTH), lambda i: (row0 + i, 0)), st_spec],
        out_shape=[jax.ShapeDtypeStruct(oa_buf.shape, F32), jax.ShapeDtypeStruct(so_buf.shape, F32)],
        input_output_aliases={7: 0, 8: 1},
        compiler_params=_cparams(("parallel",)),
        name="hgrn_sample",
    )(u, u, u, u, lbp, onorm.reshape(1, HA_DV), state, oa_buf, so_buf)


def _softplus(x):
    return jnp.maximum(x, 0.0) + jnp.log1p(jnp.exp(-jnp.abs(x)))


def _lru_gates(xc, wa_ref, wx_ref, ba, bx, sp):
    xc16 = xc.astype(BF16)
    r = _sigmoid(jnp.dot(xc16, wa_ref[...], preferred_element_type=F32) + ba)
    i = _sigmoid(jnp.dot(xc16, wx_ref[...], preferred_element_type=F32) + bx)
    log_a = -LRU_C * r * sp
    a = jnp.exp(log_a)
    th = jnp.tanh(log_a)
    mult = jnp.sqrt(2.0 * th / (th - 1.0))
    return a, mult * i * xc


def _lru_prompt_body(xb_ref, gb_ref, cw_ref, cb_ref, wa_ref, wx_ref, ba_ref, bx_ref, l_ref, nl_ref,
                     ob_ref, hl_ref, cs_ref, xe_scr, h_scr):
    t = pl.program_id(1)
    tb = LRU_TB

    @pl.when(t == 0)
    def _():
        xe_scr[0:8, :] = jnp.zeros((8, HB_WIDTH), F32)
        h_scr[...] = jnp.zeros_like(h_scr)

    xb = xb_ref[...]
    xe_scr[8:8 + tb, :] = xb
    cw = cw_ref[...]
    xc = cb_ref[...] + cw[3:4, :] * xb
    for j in range(1, CONV_W):
        xc = xc + cw[3 - j:4 - j, :] * xe_scr[pl.ds(8 - j, tb), :]
    xe_scr[0:8, :] = xb[tb - 8:tb, :]

    sp = _softplus(-l_ref[...])
    a, bt = _lru_gates(xc, wa_ref, wx_ref, ba_ref[...], bx_ref[...], sp)
    row = lax.broadcasted_iota(jnp.int32, (tb, HB_WIDTH), 0)
    sh = 1
    while sh < tb:
        keep = row >= sh
        a_sh = jnp.where(keep, pltpu.roll(a, sh, 0), 1.0)
        b_sh = jnp.where(keep, pltpu.roll(bt, sh, 0), 0.0)
        bt = a * b_sh + bt
        a = a * a_sh
        sh *= 2
    hcur = bt + a * h_scr[0:1, :]
    h_last = hcur[tb - 1:tb, :]
    h_scr[...] = jnp.broadcast_to(h_last, h_scr.shape)
    y = hcur * jax.nn.gelu(gb_ref[...])
    ob_ref[...] = _rms(y, nl_ref[...])
    hl_ref[0] = h_last
    cs_ref[0] = xb[tb - (CONV_W - 1):tb, :]


def _row(p):
    return p.reshape(1, -1)


def lru_prompt(u, cw, cb, wa, wx, ba, bx, lam, nl):
    nt = SEQ // LRU_TB
    w = HB_WIDTH

    def uspec(colblk):
        return pl.BlockSpec((LRU_TB, w), lambda b, t: (b * nt + t, colblk))

    def full(shape):
        return pl.BlockSpec(shape, lambda b, t: (0,) * len(shape))

    return pl.pallas_call(
        _lru_prompt_body,
        grid=(BATCH, nt),
        in_specs=[uspec(COL_XB), uspec(COL_GB), full((CONV_W, w)), full((1, w)), full((w, w)), full((w, w)),
                  full((1, w)), full((1, w)), full((1, w)), full((1, w))],
        out_specs=[pl.BlockSpec((LRU_TB, w), lambda b, t: (b * nt + t, 0)),
                   pl.BlockSpec((1, 1, w), lambda b, t: (b, 0, 0)),
                   pl.BlockSpec((1, CONV_W - 1, w), lambda b, t: (b, 0, 0))],
        out_shape=[jax.ShapeDtypeStruct((N_ALL, w), F32),
                   jax.ShapeDtypeStruct((BATCH, 1, w), F32),
                   jax.ShapeDtypeStruct((BATCH, CONV_W - 1, w), F32)],
        scratch_shapes=[pltpu.VMEM((LRU_TB + 8, w), F32), pltpu.VMEM((8, w), F32)],
        compiler_params=_cparams(("parallel", "arbitrary")),
        name="lru_prompt",
    )(u, u, cw, _row(cb), wa, wx, _row(ba), _row(bx), _row(lam), _row(nl))


def _lru_sample_body(xb_ref, gb_ref, h0_ref, cv_ref, cw_ref, cb_ref, wa_ref, wx_ref, ba_ref, bx_ref, l_ref, nl_ref,
                     ob_in_ref, ob_ref, hl_ref, cs_ref, x_scr, g_scr, y_scr):
    del ob_in_ref
    w = HB_WIDTH
    nb = DEC_BATCH
    nchunk = w // 128
    cw = cw_ref[...]
    for c in range(nchunk):
        x_scr[c] = xb_ref[:, c * 128:(c + 1) * 128]
        g_scr[c] = gb_ref[:, c * 128:(c + 1) * 128]

    def time_rows(scr, t):
        return jnp.concatenate([scr[c, pl.ds(t, nb, stride=DEC_SEQ), :] for c in range(nchunk)], axis=-1)

    xs = [cv_ref[0, :, j * w:(j + 1) * w] for j in range(CONV_W - 1)]
    xs += [time_rows(x_scr, t) for t in range(DEC_SEQ)]
    sp = _softplus(-l_ref[...])
    hcur = h0_ref[0]
    for t in range(DEC_SEQ):
        xc = cb_ref[...]
        for j in range(CONV_W):
            xc = xc + cw[j:j + 1, :] * xs[t + j]
        a, bt = _lru_gates(xc, wa_ref, wx_ref, ba_ref[...], bx_ref[...], sp)
        hcur = a * hcur + bt
        y = _rms(hcur * jax.nn.gelu(time_rows(g_scr, t)), nl_ref[...])
        for c in range(nchunk):
            y_scr[c, pl.ds(t, nb, stride=DEC_SEQ), :] = y[:, c * 128:(c + 1) * 128]
    for c in range(nchunk):
        ob_ref[:, c * 128:(c + 1) * 128] = y_scr[c]
    hl_ref[...] = hcur
    for j in range(CONV_W - 1):
        cs_ref[:, j * w:(j + 1) * w] = xs[DEC_SEQ + j]


def lru_sample(layer, u, h0, conv, cw, cb, wa, wx, ba, bx, lam, nl, ob_buf):
    w = HB_WIDTH
    row0 = N_P // N_S

    def full(shape):
        return pl.BlockSpec(shape, lambda i: (0,) * len(shape))

    return pl.pallas_call(
        _lru_sample_body,
        grid=(1,),
        in_specs=[pl.BlockSpec((N_S, w), lambda i: (row0, COL_XB)),
                  pl.BlockSpec((N_S, w), lambda i: (row0, COL_GB)),
                  pl.BlockSpec((1, DEC_BATCH, w), lambda i: (layer, 0, 0)),
                  pl.BlockSpec((1, DEC_BATCH, (CONV_W - 1) * w), lambda i: (layer, 0, 0)),
                  full((CONV_W, w)), full((1, w)), full((w, w)), full((w, w)),
                  full((1, w)), full((1, w)), full((1, w)), full((1, w)),
                  pl.BlockSpec(memory_space=pl.ANY)],
        out_specs=[pl.BlockSpec((N_S, w), lambda i: (row0, 0)),
                   full((DEC_BATCH, w)), full((DEC_BATCH, (CONV_W - 1) * w))],
        out_shape=[jax.ShapeDtypeStruct(ob_buf.shape, F32),
                   jax.ShapeDtypeStruct((DEC_BATCH, w), F32),
                   jax.ShapeDtypeStruct((DEC_BATCH, (CONV_W - 1) * w), F32)],
        scratch_shapes=[pltpu.VMEM((w // 128, N_S, 128), F32)] * 3,
        input_output_aliases={12: 0},
        compiler_params=_cparams(("arbitrary",)),
        name="lru_sample",
    )(u, u, h0, conv, cw, _row(cb), wa, wx, _row(ba), _row(bx), _row(lam), _row(nl), ob_buf)


XATTN_SCALE = 1.0 / math.sqrt(HC_DH)


def _xattn_prompt_body(q_ref, k_ref, v_ref, nx_ref, oc_ref):
    q = q_ref[...]
    outs = []
    for h in range(HC_HEADS):
        sl = slice(h * HC_DH, (h + 1) * HC_DH)
        s = lax.dot_general(q[:, sl].astype(BF16), k_ref[0, :, sl].astype(BF16), NT_DIMS,
                            preferred_element_type=F32) * XATTN_SCALE
        p = jnp.exp(s - jnp.max(s, axis=-1, keepdims=True))
        p = p / jnp.sum(p, axis=-1, keepdims=True)
        outs.append(jnp.dot(p.astype(BF16), v_ref[0, :, sl].astype(BF16), preferred_element_type=F32))
    oc_ref[...] = _rms(jnp.concatenate(outs, axis=-1), nx_ref[...])


def xattn_prompt(u, memkv, nx):
    nq = SEQ // XATTN_TQ
    w = HC_WIDTH
    kv = memkv.reshape(BATCH, N_MEM, 2 * w)
    return pl.pallas_call(
        _xattn_prompt_body,
        grid=(BATCH, nq),
        in_specs=[pl.BlockSpec((XATTN_TQ, w), lambda b, t: (b * nq + t, COL_QC)),
                  pl.BlockSpec((1, N_MEM, w), lambda b, t: (b, 0, 0)),
                  pl.BlockSpec((1, N_MEM, w), lambda b, t: (b, 0, 1)),
                  pl.BlockSpec((1, w), lambda b, t: (0, 0))],
        out_specs=pl.BlockSpec((XATTN_TQ, w), lambda b, t: (b * nq + t, 0)),
        out_shape=jax.ShapeDtypeStruct((N_ALL, w), F32),
        compiler_params=_cparams(("parallel", "parallel")),
        name="xattn_prompt",
    )(u, kv, kv, _row(nx))


XATTN_SB = 8


def _xattn_sample_body(q_ref, k_ref, v_ref, nx_ref, oc_in_ref, oc_ref, o_scr):
    del oc_in_ref
    q = q_ref[...]
    row8 = lax.broadcasted_iota(jnp.int32, (8, HC_DH), 0)
    for pair in range(XATTN_SB // 2):
        for h in range(HC_HEADS):
            sl = slice(h * HC_DH, (h + 1) * HC_DH)
            q8 = q[pair * 8:(pair + 1) * 8, sl].astype(BF16)
            o8 = []
            for e in range(2):
                bb = pair * 2 + e
                kh = k_ref[0, bb, pl.ds(h, N_MEM, stride=HC_HEADS), :].astype(BF16)
                vh = v_ref[0, bb, pl.ds(h, N_MEM, stride=HC_HEADS), :].astype(BF16)
                st = lax.dot_general(kh, q8, NT_DIMS, preferred_element_type=F32) * XATTN_SCALE
                pt = jnp.exp(st - jnp.max(st, axis=0, keepdims=True))
                pt = pt / jnp.sum(pt, axis=0, keepdims=True)
                o8.append(lax.dot_general(pt.astype(BF16), vh, TN_DIMS, preferred_element_type=F32))
            o_scr[pair * 8:(pair + 1) * 8, sl] = jnp.where(row8 < DEC_SEQ, o8[0], o8[1])
    oc_ref[...] = _rms(o_scr[...], nx_ref[...])


def xattn_sample(layer, u, cache_k, cache_v, nx, oc_buf):
    w = HC_WIDTH
    rows = XATTN_SB * DEC_SEQ
    row0 = N_P // rows
    kv_spec = pl.BlockSpec((1, XATTN_SB, N_MEM * HC_HEADS, HC_DH), lambda i: (layer, i, 0, 0))
    return pl.pallas_call(
        _xattn_sample_body,
        grid=(DEC_BATCH // XATTN_SB,),
        in_specs=[pl.BlockSpec((rows, w), lambda i: (row0 + i, COL_QC)), kv_spec, kv_spec,
                  pl.BlockSpec((1, w), lambda i: (0, 0)), pl.BlockSpec(memory_space=pl.ANY)],
        out_specs=pl.BlockSpec((rows, w), lambda i: (row0 + i, 0)),
        out_shape=jax.ShapeDtypeStruct(oc_buf.shape, F32),
        scratch_shapes=[pltpu.VMEM((rows, w), F32)],
        input_output_aliases={4: 0},
        compiler_params=_cparams(("parallel",)),
        name="xattn_sample",
    )(u, cache_k, cache_v, _row(nx), oc_buf)


POST_TM = 512
ROUTE_LANES = 128


def _post_body(oa_ref, ob_ref, oc_ref, x_ref, w_ref, nf_ref, rw_ref, rb_ref, x1_ref, hn_ref, ti_ref, tg_ref):
    acc = jnp.dot(oa_ref[...].astype(BF16), w_ref[0:HA_WIDTH, :], preferred_element_type=F32)
    acc += jnp.dot(ob_ref[...].astype(BF16), w_ref[HA_WIDTH:HA_WIDTH + HB_WIDTH, :], preferred_element_type=F32)
    acc += jnp.dot(oc_ref[...].astype(BF16), w_ref[HA_WIDTH + HB_WIDTH:, :], preferred_element_type=F32)
    x1 = x_ref[...] + acc
    x1_ref[...] = x1
    hn = _rms(x1, nf_ref[...])
    hn_ref[...] = hn.astype(BF16)
    logits = jnp.dot(hn, rw_ref[...], preferred_element_type=F32, precision=lax.Precision.HIGHEST) + rb_ref[...]
    tm = logits.shape[0]
    col = lax.broadcasted_iota(jnp.int32, (tm, N_EXPERTS), 1).astype(F32)
    lane = lax.broadcasted_iota(jnp.int32, (tm, ROUTE_LANES), 1)
    work = logits
    ti = jnp.zeros((tm, ROUTE_LANES), jnp.int32)
    tv = jnp.zeros((tm, ROUTE_LANES), F32)
    vals = []
    for kk in range(TOP_K):
        m = jnp.max(work, axis=-1, keepdims=True)
        idx = jnp.min(jnp.where(work == m, col, float(N_EXPERTS)), axis=-1, keepdims=True)
        work = jnp.where(col == idx, -jnp.inf, work)
        vals.append(m)
        ti = jnp.where(lane == kk, idx.astype(jnp.int32), ti)
    es = [jnp.exp(m - vals[0]) for m in vals]
    tot = functools.reduce(lambda a, b: a + b, es)
    for kk in range(TOP_K):
        tv = jnp.where(lane == kk, es[kk] / tot, tv)
    ti_ref[...] = ti
    tg_ref[...] = tv


def post_mixer(oa, ob, oc, x, w_out, nf, rw, rb):
    tm = POST_TM
    d = D_MODEL

    def rows(width):
        return pl.BlockSpec((tm, width), lambda i: (i, 0))

    def full(shape):
        return pl.BlockSpec(shape, lambda i: (0,) * len(shape))

    return pl.pallas_call(
        _post_body,
        grid=(N_ALL // tm,),
        in_specs=[rows(HA_WIDTH), rows(HB_WIDTH), rows(HC_WIDTH), rows(d), full((d, d)), full((1, d)),
                  full((d, N_EXPERTS)), full((1, N_EXPERTS))],
        out_specs=[rows(d), rows(d), rows(ROUTE_LANES), rows(ROUTE_LANES)],
        out_shape=[jax.ShapeDtypeStruct((N_ALL, d), F32), jax.ShapeDtypeStruct((N_ALL, d), BF16),
                   jax.ShapeDtypeStruct((N_ALL, ROUTE_LANES), jnp.int32),
                   jax.ShapeDtypeStruct((N_ALL, ROUTE_LANES), F32)],
        compiler_params=_cparams(("parallel",)),
        name="post_mixer",
    )(oa, ob, oc, x, w_out, _row(nf), rw, _row(rb))


def _gmm1_body(te_ref, nv_ref, x_ref, wg_ref, wl_ref, bg_ref, bl_ref, o_ref):
    del te_ref

    @pl.when(pl.program_id(1) < nv_ref[0])
    def _():
        x = x_ref[...]
        glu = jnp.dot(x, wg_ref[0].astype(BF16), preferred_element_type=F32) + bg_ref[0]
        lin = jnp.dot(x, wl_ref[0].astype(BF16), preferred_element_type=F32) + bl_ref[0]
        glu = jnp.minimum(glu, SWIGLU_LIMIT)
        lin = jnp.clip(lin, -SWIGLU_LIMIT, SWIGLU_LIMIT)
        o_ref[...] = (glu * _sigmoid(SWIGLU_ALPHA * glu) * (lin + 1.0)).astype(BF16)


def _gmm2_body(te_ref, nv_ref, a_ref, w_ref, b_ref, o_ref):
    del te_ref

    @pl.when(pl.program_id(1) < nv_ref[0])
    def _():
        o_ref[...] = jnp.dot(a_ref[...], w_ref[0].astype(BF16), preferred_element_type=F32) + b_ref[0]


def _tile_row(t, nv):
    return jnp.minimum(t, nv[0] - 1)


def moe_gmm1(xs, w1, b1, tile_expert, n_valid):
    tm, tn = MOE_TM, MOE_TN1
    nb = D_FF // tn
    d = D_MODEL
    grid_spec = pltpu.PrefetchScalarGridSpec(
        num_scalar_prefetch=2,
        grid=(nb, MOE_TILES),
        in_specs=[
            pl.BlockSpec((tm, d), lambda n, t, te, nv: (_tile_row(t, nv), 0)),
            pl.BlockSpec((1, d, tn), lambda n, t, te, nv: (te[t], 0, n)),
            pl.BlockSpec((1, d, tn), lambda n, t, te, nv: (te[t], 0, n + nb)),
            pl.BlockSpec((1, 1, tn), lambda n, t, te, nv: (te[t], 0, n)),
            pl.BlockSpec((1, 1, tn), lambda n, t, te, nv: (te[t], 0, n + nb)),
        ],
        out_specs=pl.BlockSpec((tm, tn), lambda n, t, te, nv: (_tile_row(t, nv), n)),
    )
    b1r = b1.reshape(N_EXPERTS, 1, 2 * D_FF)
    return pl.pallas_call(
        _gmm1_body,
        grid_spec=grid_spec,
        out_shape=jax.ShapeDtypeStruct((MOE_ROWS, D_FF), BF16),
        compiler_params=_cparams(("parallel", "arbitrary")),
        name="moe_gmm1",
    )(tile_expert, n_valid, xs, w1, w1, b1r, b1r)


def moe_gmm2(act, w2, b2, tile_expert, n_valid):
    tm, tn = MOE_TM, MOE_TN2
    nb = D_MODEL // tn
    grid_spec = pltpu.PrefetchScalarGridSpec(
        num_scalar_prefetch=2,
        grid=(nb, MOE_TILES),
        in_specs=[
            pl.BlockSpec((tm, D_FF), lambda n, t, te, nv: (_tile_row(t, nv), 0)),
            pl.BlockSpec((1, D_FF, tn), lambda n, t, te, nv: (te[t], 0, n)),
            pl.BlockSpec((1, 1, tn), lambda n, t, te, nv: (te[t], 0, n)),
        ],
        out_specs=pl.BlockSpec((tm, tn), lambda n, t, te, nv: (_tile_row(t, nv), n)),
    )
    return pl.pallas_call(
        _gmm2_body,
        grid_spec=grid_spec,
        out_shape=jax.ShapeDtypeStruct((MOE_ROWS, D_MODEL), F32),
        compiler_params=_cparams(("parallel", "arbitrary")),
        name="moe_gmm2",
    )(tile_expert, n_valid, act, w2, b2.reshape(N_EXPERTS, 1, D_MODEL))


def moe_layout(ti):
    flat_e = ti.reshape(-1)
    oh = (flat_e[:, None] == jnp.arange(N_EXPERTS, dtype=jnp.int32)[None, :]).astype(jnp.int32)
    csum = jnp.cumsum(oh, axis=0)
    counts = csum[-1]
    rank = jnp.sum((csum - oh) * oh, axis=1)
    tiles_e = (counts + MOE_TM - 1) // MOE_TM
    tile_end = jnp.cumsum(tiles_e)
    row_off = (tile_end - tiles_e) * MOE_TM
    pos = row_off[flat_e] + rank
    tile_expert = jnp.minimum(
        jnp.searchsorted(tile_end, jnp.arange(MOE_TILES, dtype=jnp.int32), side="right"), N_EXPERTS - 1
    ).astype(jnp.int32)
    n_valid = tile_end[-1:].astype(jnp.int32)
    return pos.astype(jnp.int32), tile_expert, n_valid


def _final_norm_body(x_ref, g_ref, o_ref):
    o_ref[...] = _rms(x_ref[...], g_ref[...])


def final_norm(x, g):
    tm = POST_TM
    return pl.pallas_call(
        _final_norm_body,
        grid=(N_ALL // tm,),
        in_specs=[pl.BlockSpec((tm, D_MODEL), lambda i: (i, 0)), pl.BlockSpec((1, D_MODEL), lambda i: (0, 0))],
        out_specs=pl.BlockSpec((tm, D_MODEL), lambda i: (i, 0)),
        out_shape=jax.ShapeDtypeStruct((N_ALL, D_MODEL), F32),
        compiler_params=_cparams(("parallel",)),
        name="final_norm",
    )(x, _row(g))


def _block_diag(w):
    eye = jnp.eye(HB_BLOCKS, dtype=w.dtype)
    return jnp.einsum("hij,hg->higj", w, eye).reshape(HB_WIDTH, HB_WIDTH)


def kernel(x_prompt, x_sample, mem_prompt, state_hgrn, state_lru, state_conv, cache_mem_k, cache_mem_v,
           norm_mix, w_in, hgrn_lb, hgrn_onorm, conv_w, conv_b, lru_wa, lru_ba, lru_wx, lru_bx, lru_L,
           norm_lru, norm_xattn, norm_mem, w_mem_k, w_mem_v, w_out, norm_ffn, router_w, router_b,
           moe_w1, moe_b1, moe_w2, moe_b2, norm_final):
    x = jnp.concatenate([x_prompt.reshape(N_P, D_MODEL), x_sample.reshape(N_S, D_MODEL)], axis=0)
    mem = mem_prompt.reshape(BATCH * N_MEM, D_MODEL)
    cache_k = cache_mem_k.reshape(DEPTH, DEC_BATCH, N_MEM * HC_HEADS, HC_DH)
    cache_v = cache_mem_v.reshape(DEPTH, DEC_BATCH, N_MEM * HC_HEADS, HC_DH)
    conv_s = state_conv.reshape(DEPTH, DEC_BATCH, (CONV_W - 1) * HB_WIDTH)
    hgrn_s_out = jnp.zeros(state_hgrn.shape, F32)

    p_hgrn, p_lru, p_conv, p_mk, p_mv, s_lru, s_conv = [], [], [], [], [], [], []
    for l in range(DEPTH):
        w_mem = jnp.concatenate([w_mem_k[l], w_mem_v[l]], axis=1).astype(BF16)
        memkv = norm_matmul(mem, norm_mem[l], w_mem, tm=BATCH * N_MEM, tn=512)
        p_mk.append(memkv[:, :HC_WIDTH].reshape(BATCH, N_MEM, HC_HEADS, HC_DH))
        p_mv.append(memkv[:, HC_WIDTH:].reshape(BATCH, N_MEM, HC_HEADS, HC_DH))

        u = norm_matmul(x, norm_mix[l], w_in[l].astype(BF16), tm=N_ALL // 8, tn=512)

        oa, hg_p = hgrn_prompt(l, u, hgrn_lb, hgrn_onorm[l])
        oa, hgrn_s_out = hgrn_sample(l, u, hgrn_lb, hgrn_onorm[l], state_hgrn, oa, hgrn_s_out)

        wa = _block_diag(lru_wa[l]).astype(BF16)
        wx = _block_diag(lru_wx[l]).astype(BF16)
        lru_args = (conv_w[l], conv_b[l], wa, wx, lru_ba[l], lru_bx[l], lru_L[l], norm_lru[l])
        ob, hl_p, cs_p = lru_prompt(u, *lru_args)
        ob, hl_s, cs_s = lru_sample(l, u, state_lru, conv_s, *lru_args, ob)

        oc = xattn_prompt(u, memkv, norm_xattn[l])
        oc = xattn_sample(l, u, cache_k, cache_v, norm_xattn[l], oc)

        x1, hn, ti, tg = post_mixer(oa, ob, oc, x, w_out[l].astype(BF16), norm_ffn[l], router_w[l], router_b[l])
        ti = ti[:, :TOP_K]
        tg = tg[:, :TOP_K]
        pos, tile_expert, n_valid = moe_layout(ti)
        tok_of_slot = jnp.zeros((MOE_ROWS,), jnp.int32).at[pos].set(
            jnp.arange(N_ALL * TOP_K, dtype=jnp.int32) // TOP_K)
        xs = hn[tok_of_slot]
        act = moe_gmm1(xs, moe_w1[l], moe_b1[l], tile_expert, n_valid)
        yo = moe_gmm2(act, moe_w2[l], moe_b2[l], tile_expert, n_valid)
        y = jnp.sum(yo[pos].reshape(N_ALL, TOP_K, D_MODEL) * tg[:, :, None], axis=1)
        x = x1 + y

        p_hgrn.append(hg_p)
        p_lru.append(hl_p.reshape(BATCH, HB_WIDTH))
        p_conv.append(cs_p)
        s_lru.append(hl_s)
        s_conv.append(cs_s.reshape(DEC_BATCH, CONV_W - 1, HB_WIDTH))

    y_all = final_norm(x, norm_final)
    y_prompt = y_all[:N_P].reshape(BATCH, SEQ, D_MODEL)
    y_sample = y_all[N_P:].reshape(DEC_BATCH, DEC_SEQ, D_MODEL)
    return (y_prompt, y_sample, jnp.stack(p_hgrn), jnp.stack(p_lru), jnp.stack(p_conv),
            jnp.stack(p_mk), jnp.stack(p_mv), hgrn_s_out, jnp.stack(s_lru), jnp.stack(s_conv))
```

```python
import functools
import math

import jax
import jax.numpy as jnp
from jax import lax
from jax.experimental import pallas as pl
from jax.experimental.pallas import tpu as pltpu

F32 = jnp.float32
BF16 = jnp.bfloat16

D_MODEL = 2048
BATCH = 4
SEQ = 2048
DEPTH = 2
DEC_BATCH = 128
DEC_SEQ = 4
HA_DK = 128
HA_WIDTH = D_MODEL // 2
HA_HEADS = HA_WIDTH // HA_DK
HA_DV = HA_WIDTH // HA_HEADS
HB_WIDTH = D_MODEL // 4
HB_BLOCKS = 8
HB_BW = HB_WIDTH // HB_BLOCKS
CONV_W = 4
LRU_C = 8.0
HC_HEADS = 4
HC_WIDTH = D_MODEL - HA_WIDTH - HB_WIDTH
HC_DH = HC_WIDTH // HC_HEADS
N_MEM = 256
IN_COLS = 4 * HA_WIDTH + 2 * HB_WIDTH + HC_WIDTH
N_EXPERTS = 32
TOP_K = 4
D_FF = D_MODEL
SWIGLU_LIMIT = 7.0
SWIGLU_ALPHA = 1.702
EPS = 1e-6

N_P = BATCH * SEQ
N_S = DEC_BATCH * DEC_SEQ
N_ALL = N_P + N_S

COL_QA, COL_FA, COL_IA, COL_GA = 0, 1, 2, 3
COL_XB, COL_GB, COL_QC = 8, 9, 10

HGRN_CHUNK = 64
HGRN_SUB = 16
LRU_TB = 256
XATTN_TQ = 512
MOE_TM = 512
MOE_TN1 = 512
MOE_TN2 = 512
MOE_TILES = (N_ALL * TOP_K) // MOE_TM + N_EXPERTS
MOE_ROWS = MOE_TILES * MOE_TM
VMEM_LIMIT = 56 * 1024 * 1024

NT_DIMS = (((1,), (1,)), ((), ()))
TN_DIMS = (((0,), (0,)), ((), ()))


def _cparams(sem):
    return pltpu.CompilerParams(dimension_semantics=sem, vmem_limit_bytes=VMEM_LIMIT)


def _rms(x, g):
    return x * lax.rsqrt(jnp.mean(x * x, axis=-1, keepdims=True) + EPS) * g


def _sigmoid(x):
    return jax.nn.sigmoid(x)


def _silu(x):
    return x * jax.nn.sigmoid(x)


def _norm_matmul_body(x_ref, g_ref, w_ref, o_ref, xn_ref):
    @pl.when(pl.program_id(1) == 0)
    def _():
        xn_ref[...] = _rms(x_ref[...], g_ref[...]).astype(BF16)

    o_ref[...] = jnp.dot(xn_ref[...], w_ref[...], preferred_element_type=F32)


def norm_matmul(x, g, w, tm, tn):
    n, d = x.shape
    nc = w.shape[1]
    return pl.pallas_call(
        _norm_matmul_body,
        grid=(n // tm, nc // tn),
        in_specs=[
            pl.BlockSpec((tm, d), lambda i, j: (i, 0)),
            pl.BlockSpec((1, d), lambda i, j: (0, 0)),
            pl.BlockSpec((d, tn), lambda i, j: (0, j)),
        ],
        out_specs=pl.BlockSpec((tm, tn), lambda i, j: (i, j)),
        out_shape=jax.ShapeDtypeStruct((n, nc), F32),
        scratch_shapes=[pltpu.VMEM((tm, d), BF16)],
        compiler_params=_cparams(("parallel", "arbitrary")),
        name="norm_matmul",
    )(x, g.reshape(1, d), w)


def _hgrn_lower_bound(lbp, layer):
    rows = [lbp[r:r + 1, :] for r in range(DEPTH)]
    m = functools.reduce(jnp.maximum, rows)
    es = [jnp.exp(r - m) for r in rows]
    tot = functools.reduce(lambda a, b: a + b, es)
    lb = jnp.zeros_like(m)
    for r in range(1, layer + 1):
        lb = lb + es[r] / tot
    return lb


def _hgrn_gates(z, qa, lb):
    f = lb + (1.0 - lb) * _sigmoid(z)
    k = (1.0 - lb) * _sigmoid(-z)
    q = _silu(qa)
    return f, k, q


def _hgrn_prompt_body(layer, qa_ref, fa_ref, ia_ref, ga_ref, lbp_ref, on_ref, oa_ref, st_ref, s_scr):
    c = pl.program_id(1)
    C, SUB = HGRN_CHUNK, HGRN_SUB
    nsub = C // SUB

    @pl.when(c == 0)
    def _():
        s_scr[...] = jnp.zeros_like(s_scr)

    lb = _hgrn_lower_bound(lbp_ref[...], layer)
    f, k, q = _hgrn_gates(fa_ref[...], qa_ref[...], lb)
    g = jnp.log(f)
    v = ia_ref[...]
    gate = _silu(ga_ref[...])

    row = lax.broadcasted_iota(jnp.int32, (C, C), 0)
    col = lax.broadcasted_iota(jnp.int32, (C, C), 1)
    tri = (row >= col).astype(F32)
    b_all = jnp.dot(tri, g, preferred_element_type=F32, precision=lax.Precision.HIGHEST)

    row_sub = lax.broadcasted_iota(jnp.int32, (SUB, HA_DK), 0)
    row_c = lax.broadcasted_iota(jnp.int32, (C, HA_DK), 0)
    lane_c = lax.broadcasted_iota(jnp.int32, (SUB, C), 1)
    neg_inf = jnp.float32(-jnp.inf)

    states = [s_scr[h] for h in range(HA_HEADS)]
    new_states, outs = [], []
    for h in range(HA_HEADS):
        sl = slice(h * HA_DK, (h + 1) * HA_DK)
        bh, qh, kh, vh = b_all[:, sl], q[:, sl], k[:, sl], v[:, sl]
        kh16 = kh.astype(BF16)
        vh16 = vh.astype(BF16)
        b_end = bh[C - 1:C, :]
        st = states[h]
        o = lax.dot_general((qh * jnp.exp(bh)).astype(BF16), st.astype(BF16), NT_DIMS,
                            preferred_element_type=F32)
        kdec = (kh * jnp.exp(b_end - bh)).astype(BF16)
        upd = lax.dot_general(vh16, kdec, TN_DIMS, preferred_element_type=F32)
        new_states.append(st * jnp.exp(b_end) + upd)

        a_rows = []
        for i in range(nsub):
            bi = bh[i * SUB:(i + 1) * SUB, :]
            qi = qh[i * SUB:(i + 1) * SUB, :]
            parts = []
            for s in range(SUB):
                dec = jnp.exp(jnp.where(row_sub >= s, bi - bi[s:s + 1, :], neg_inf))
                parts.append(qi * dec)
            q_all = jnp.concatenate(parts, axis=0).astype(BF16)
            m = lax.dot_general(q_all, kh16, NT_DIMS, preferred_element_type=F32)
            a_i = jnp.zeros((SUB, C), F32)
            for s in range(SUB):
                a_i = a_i + jnp.where(lane_c == i * SUB + s, m[s * SUB:(s + 1) * SUB, :], 0.0)
            if i > 0:
                b_prev = bh[i * SUB - 1:i * SUB, :]
                q_i = (qi * jnp.exp(bi - b_prev)).astype(BF16)
                k_i = (kh * jnp.exp(jnp.where(row_c < i * SUB, b_prev - bh, neg_inf))).astype(BF16)
                a_i = a_i + lax.dot_general(q_i, k_i, NT_DIMS, preferred_element_type=F32)
            a_rows.append(a_i)
        att = jnp.concatenate(a_rows, axis=0).astype(BF16)
        o = o + jnp.dot(att, vh16, preferred_element_type=F32)
        outs.append(_rms(o, on_ref[...]) * gate[:, sl])

    for h in range(HA_HEADS):
        oa_ref[:, h * HA_DK:(h + 1) * HA_DK] = outs[h]
        s_scr[h] = new_states[h]

    @pl.when(c == pl.num_programs(1) - 1)
    def _():
        for h in range(HA_HEADS):
            st_ref[0, h] = new_states[h].T


def hgrn_prompt(layer, u, lbp, onorm):
    nchunk = SEQ // HGRN_CHUNK

    def spec(colblk):
        return pl.BlockSpec((HGRN_CHUNK, HA_WIDTH), lambda b, c: (b * nchunk + c, colblk))

    return pl.pallas_call(
        functools.partial(_hgrn_prompt_body, layer),
        grid=(BATCH, nchunk),
        in_specs=[spec(COL_QA), spec(COL_FA), spec(COL_IA), spec(COL_GA),
                  pl.BlockSpec((DEPTH, HA_WIDTH), lambda b, c: (0, 0)),
                  pl.BlockSpec((1, HA_DV), lambda b, c: (0, 0))],
        out_specs=[pl.BlockSpec((HGRN_CHUNK, HA_WIDTH), lambda b, c: (b * nchunk + c, 0)),
                   pl.BlockSpec((1, HA_HEADS, HA_DK, HA_DV), lambda b, c: (b, 0, 0, 0))],
        out_shape=[jax.ShapeDtypeStruct((N_P, HA_WIDTH), F32),
                   jax.ShapeDtypeStruct((BATCH, HA_HEADS, HA_DK, HA_DV), F32)],
        scratch_shapes=[pltpu.VMEM((HA_HEADS, HA_DV, HA_DK), F32)],
        compiler_params=_cparams(("parallel", "arbitrary")),
        name="hgrn_prompt",
    )(u, u, u, u, lbp, onorm.reshape(1, HA_DV))


HGRN_SB = 2


def _hgrn_sample_body(layer, qa_ref, fa_ref, ia_ref, ga_ref, lbp_ref, on_ref, s_ref, so_in_ref, oa_ref, so_ref):
    del so_in_ref
    lb = _hgrn_lower_bound(lbp_ref[...], layer)
    f, k, q = _hgrn_gates(fa_ref[...], qa_ref[...], lb)
    v = ia_ref[...]
    gate = _silu(ga_ref[...])
    rows8 = HGRN_SB * DEC_SEQ
    for h in range(HA_HEADS):
        sl = slice(h * HA_DK, (h + 1) * HA_DK)
        ft, kt, qt = f[:, sl].T, k[:, sl].T, q[:, sl].T
        outs = []
        for bb in range(HGRN_SB):
            s = s_ref[0, bb, h]
            for t in range(DEC_SEQ):
                j = bb * DEC_SEQ + t
                s = ft[:, j:j + 1] * s + kt[:, j:j + 1] * v[j:j + 1, sl]
                outs.append(jnp.sum(qt[:, j:j + 1] * s, axis=0, keepdims=True))
            so_ref[0, bb, h] = s
        o = jnp.concatenate(outs, axis=0)
        oa_ref[:, sl] = _rms(o, on_ref[...]) * gate[:, sl]


def hgrn_sample(layer, u, lbp, onorm, state, so_buf):
    rows8 = HGRN_SB * DEC_SEQ
    row0 = N_P // rows8

    def spec(colblk):
        return pl.BlockSpec((rows8, HA_WIDTH), lambda i: (row0 + i, colblk))

    st_spec = pl.BlockSpec((1, HGRN_SB, HA_HEADS, HA_DK, HA_DV), lambda i: (layer, i, 0, 0, 0))
    return pl.pallas_call(
        functools.partial(_hgrn_sample_body, layer),
        grid=(DEC_BATCH // HGRN_SB,),
        in_specs=[spec(COL_QA), spec(COL_FA), spec(COL_IA), spec(COL_GA),
                  pl.BlockSpec((DEPTH, HA_WIDTH), lambda i: (0, 0)),
                  pl.BlockSpec((1, HA_DV), lambda i: (0, 0)),
                  st_spec, pl.BlockSpec(memory_space=pl.ANY)],
        out_specs=[pl.BlockSpec((rows8, HA_WIDTH), lambda i: (i, 0)), st_spec],
        out_shape=[jax.ShapeDtypeStruct((N_S, HA_WIDTH), F32), jax.ShapeDtypeStruct(state.shape, F32)],
        input_output_aliases={7: 1},
        compiler_params=_cparams(("parallel",)),
        name="hgrn_sample",
    )(u, u, u, u, lbp, onorm.reshape(1, HA_DV), state, so_buf)


def _softplus(x):
    return jnp.maximum(x, 0.0) + jnp.log1p(jnp.exp(-jnp.abs(x)))


def _lru_gates(xc, wa_ref, wx_ref, ba, bx, sp):
    xc16 = xc.astype(BF16)
    r = _sigmoid(jnp.dot(xc16, wa_ref[...], preferred_element_type=F32) + ba)
    i = _sigmoid(jnp.dot(xc16, wx_ref[...], preferred_element_type=F32) + bx)
    log_a = -LRU_C * r * sp
    a = jnp.exp(log_a)
    th = jnp.tanh(log_a)
    mult = jnp.sqrt(2.0 * th / (th - 1.0))
    return a, mult * i * xc


def _lru_prompt_body(xb_ref, gb_ref, cw_ref, cb_ref, wa_ref, wx_ref, ba_ref, bx_ref, l_ref, nl_ref,
                     ob_ref, hl_ref, cs_ref, xe_scr, h_scr):
    t = pl.program_id(1)
    tb = LRU_TB

    @pl.when(t == 0)
    def _():
        xe_scr[0:8, :] = jnp.zeros((8, HB_WIDTH), F32)
        h_scr[...] = jnp.zeros_like(h_scr)

    xb = xb_ref[...]
    xe_scr[8:8 + tb, :] = xb
    cw = cw_ref[...]
    xc = cb_ref[...] + cw[3:4, :] * xb
    for j in range(1, CONV_W):
        xc = xc + cw[3 - j:4 - j, :] * xe_scr[pl.ds(8 - j, tb), :]
    xe_scr[0:8, :] = xb[tb - 8:tb, :]

    sp = _softplus(-l_ref[...])
    a, bt = _lru_gates(xc, wa_ref, wx_ref, ba_ref[...], bx_ref[...], sp)
    row = lax.broadcasted_iota(jnp.int32, (tb, HB_WIDTH), 0)
    sh = 1
    while sh < tb:
        keep = row >= sh
        a_sh = jnp.where(keep, pltpu.roll(a, sh, 0), 1.0)
        b_sh = jnp.where(keep, pltpu.roll(bt, sh, 0), 0.0)
        bt = a * b_sh + bt
        a = a * a_sh
        sh *= 2
    hcur = bt + a * h_scr[0:1, :]
    h_last = hcur[tb - 1:tb, :]
    h_scr[...] = jnp.broadcast_to(h_last, h_scr.shape)
    y = hcur * jax.nn.gelu(gb_ref[...])
    ob_ref[...] = _rms(y, nl_ref[...])
    hl_ref[0] = h_last
    cs_ref[0] = xb[tb - (CONV_W - 1):tb, :]


def _row(p):
    return p.reshape(1, -1)


def lru_prompt(u, cw, cb, wa, wx, ba, bx, lam, nl):
    nt = SEQ // LRU_TB
    w = HB_WIDTH

    def uspec(colblk):
        return pl.BlockSpec((LRU_TB, w), lambda b, t: (b * nt + t, colblk))

    def full(shape):
        return pl.BlockSpec(shape, lambda b, t: (0,) * len(shape))

    return pl.pallas_call(
        _lru_prompt_body,
        grid=(BATCH, nt),
        in_specs=[uspec(COL_XB), uspec(COL_GB), full((CONV_W, w)), full((1, w)), full((w, w)), full((w, w)),
                  full((1, w)), full((1, w)), full((1, w)), full((1, w))],
        out_specs=[pl.BlockSpec((LRU_TB, w), lambda b, t: (b * nt + t, 0)),
                   pl.BlockSpec((1, 1, w), lambda b, t: (b, 0, 0)),
                   pl.BlockSpec((1, CONV_W - 1, w), lambda b, t: (b, 0, 0))],
        out_shape=[jax.ShapeDtypeStruct((N_P, w), F32),
                   jax.ShapeDtypeStruct((BATCH, 1, w), F32),
                   jax.ShapeDtypeStruct((BATCH, CONV_W - 1, w), F32)],
        scratch_shapes=[pltpu.VMEM((LRU_TB + 8, w), F32), pltpu.VMEM((8, w), F32)],
        compiler_params=_cparams(("parallel", "arbitrary")),
        name="lru_prompt",
    )(u, u, cw, _row(cb), wa, wx, _row(ba), _row(bx), _row(lam), _row(nl))


def _lru_sample_body(xb_ref, gb_ref, h0_ref, cv_ref, cw_ref, cb_ref, wa_ref, wx_ref, ba_ref, bx_ref, l_ref, nl_ref,
                     ob_ref, hl_ref, cs_ref, x_scr, g_scr, y_scr):
    w = HB_WIDTH
    nb = DEC_BATCH
    nchunk = w // 128
    cw = cw_ref[...]
    for c in range(nchunk):
        x_scr[c] = xb_ref[:, c * 128:(c + 1) * 128]
        g_scr[c] = gb_ref[:, c * 128:(c + 1) * 128]

    def time_rows(scr, t):
        return jnp.concatenate([scr[c, pl.ds(t, nb, stride=DEC_SEQ), :] for c in range(nchunk)], axis=-1)

    xs = [cv_ref[0, :, j * w:(j + 1) * w] for j in range(CONV_W - 1)]
    xs += [time_rows(x_scr, t) for t in range(DEC_SEQ)]
    sp = _softplus(-l_ref[...])
    hcur = h0_ref[0]
    for t in range(DEC_SEQ):
        xc = cb_ref[...]
        for j in range(CONV_W):
            xc = xc + cw[j:j + 1, :] * xs[t + j]
        a, bt = _lru_gates(xc, wa_ref, wx_ref, ba_ref[...], bx_ref[...], sp)
        hcur = a * hcur + bt
        y = _rms(hcur * jax.nn.gelu(time_rows(g_scr, t)), nl_ref[...])
        for c in range(nchunk):
            y_scr[c, pl.ds(t, nb, stride=DEC_SEQ), :] = y[:, c * 128:(c + 1) * 128]
    for c in range(nchunk):
        ob_ref[:, c * 128:(c + 1) * 128] = y_scr[c]
    hl_ref[...] = hcur
    for j in range(CONV_W - 1):
        cs_ref[:, j * w:(j + 1) * w] = xs[DEC_SEQ + j]


def lru_sample(layer, u, h0, conv, cw, cb, wa, wx, ba, bx, lam, nl):
    w = HB_WIDTH
    row0 = N_P // N_S

    def full(shape):
        return pl.BlockSpec(shape, lambda i: (0,) * len(shape))

    return pl.pallas_call(
        _lru_sample_body,
        grid=(1,),
        in_specs=[pl.BlockSpec((N_S, w), lambda i: (row0, COL_XB)),
                  pl.BlockSpec((N_S, w), lambda i: (row0, COL_GB)),
                  pl.BlockSpec((1, DEC_BATCH, w), lambda i: (layer, 0, 0)),
                  pl.BlockSpec((1, DEC_BATCH, (CONV_W - 1) * w), lambda i: (layer, 0, 0)),
                  full((CONV_W, w)), full((1, w)), full((w, w)), full((w, w)),
                  full((1, w)), full((1, w)), full((1, w)), full((1, w))],
        out_specs=[full((N_S, w)), full((DEC_BATCH, w)), full((DEC_BATCH, (CONV_W - 1) * w))],
        out_shape=[jax.ShapeDtypeStruct((N_S, w), F32),
                   jax.ShapeDtypeStruct((DEC_BATCH, w), F32),
                   jax.ShapeDtypeStruct((DEC_BATCH, (CONV_W - 1) * w), F32)],
        scratch_shapes=[pltpu.VMEM((w // 128, N_S, 128), F32)] * 3,
        compiler_params=_cparams(("arbitrary",)),
        name="lru_sample",
    )(u, u, h0, conv, cw, _row(cb), wa, wx, _row(ba), _row(bx), _row(lam), _row(nl))


XATTN_SCALE = 1.0 / math.sqrt(HC_DH)


def _xattn_prompt_body(q_ref, k_ref, v_ref, nx_ref, oc_ref):
    q = q_ref[...]
    outs = []
    for h in range(HC_HEADS):
        sl = slice(h * HC_DH, (h + 1) * HC_DH)
        s = lax.dot_general(q[:, sl].astype(BF16), k_ref[0, :, sl].astype(BF16), NT_DIMS,
                            preferred_element_type=F32) * XATTN_SCALE
        p = jnp.exp(s - jnp.max(s, axis=-1, keepdims=True))
        p = p / jnp.sum(p, axis=-1, keepdims=True)
        outs.append(jnp.dot(p.astype(BF16), v_ref[0, :, sl].astype(BF16), preferred_element_type=F32))
    oc_ref[...] = _rms(jnp.concatenate(outs, axis=-1), nx_ref[...])


def xattn_prompt(u, memkv, nx):
    nq = SEQ // XATTN_TQ
    w = HC_WIDTH
    kv = memkv.reshape(BATCH, N_MEM, 2 * w)
    return pl.pallas_call(
        _xattn_prompt_body,
        grid=(BATCH, nq),
        in_specs=[pl.BlockSpec((XATTN_TQ, w), lambda b, t: (b * nq + t, COL_QC)),
                  pl.BlockSpec((1, N_MEM, w), lambda b, t: (b, 0, 0)),
                  pl.BlockSpec((1, N_MEM, w), lambda b, t: (b, 0, 1)),
                  pl.BlockSpec((1, w), lambda b, t: (0, 0))],
        out_specs=pl.BlockSpec((XATTN_TQ, w), lambda b, t: (b * nq + t, 0)),
        out_shape=jax.ShapeDtypeStruct((N_P, w), F32),
        compiler_params=_cparams(("parallel", "parallel")),
        name="xattn_prompt",
    )(u, kv, kv, _row(nx))


XATTN_SB = 8


def _xattn_sample_body(q_ref, k_ref, v_ref, nx_ref, oc_ref, o_scr):
    q = q_ref[...]
    row8 = lax.broadcasted_iota(jnp.int32, (8, HC_DH), 0)
    for pair in range(XATTN_SB // 2):
        for h in range(HC_HEADS):
            sl = slice(h * HC_DH, (h + 1) * HC_DH)
            q8 = q[pair * 8:(pair + 1) * 8, sl].astype(BF16)
            o8 = []
            for e in range(2):
                bb = pair * 2 + e
                kh = k_ref[0, bb, pl.ds(h, N_MEM, stride=HC_HEADS), :].astype(BF16)
                vh = v_ref[0, bb, pl.ds(h, N_MEM, stride=HC_HEADS), :].astype(BF16)
                st = lax.dot_general(kh, q8, NT_DIMS, preferred_element_type=F32) * XATTN_SCALE
                pt = jnp.exp(st - jnp.max(st, axis=0, keepdims=True))
                pt = pt / jnp.sum(pt, axis=0, keepdims=True)
                o8.append(lax.dot_general(pt.astype(BF16), vh, TN_DIMS, preferred_element_type=F32))
            o_scr[pair * 8:(pair + 1) * 8, sl] = jnp.where(row8 < DEC_SEQ, o8[0], o8[1])
    oc_ref[...] = _rms(o_scr[...], nx_ref[...])


def xattn_sample(layer, u, cache_k, cache_v, nx):
    w = HC_WIDTH
    rows = XATTN_SB * DEC_SEQ
    row0 = N_P // rows
    kv_spec = pl.BlockSpec((1, XATTN_SB, N_MEM * HC_HEADS, HC_DH), lambda i: (layer, i, 0, 0))
    return pl.pallas_call(
        _xattn_sample_body,
        grid=(DEC_BATCH // XATTN_SB,),
        in_specs=[pl.BlockSpec((rows, w), lambda i: (row0 + i, COL_QC)), kv_spec, kv_spec,
                  pl.BlockSpec((1, w), lambda i: (0, 0))],
        out_specs=pl.BlockSpec((rows, w), lambda i: (i, 0)),
        out_shape=jax.ShapeDtypeStruct((N_S, w), F32),
        scratch_shapes=[pltpu.VMEM((rows, w), F32)],
        compiler_params=_cparams(("parallel",)),
        name="xattn_sample",
    )(u, cache_k, cache_v, _row(nx))


POST_TM = 512
ROUTE_LANES = 128
PACK_W = D_MODEL // 2


def _post_body(oap_ref, obp_ref, ocp_ref, oas_ref, obs_ref, ocs_ref, x_ref, w_ref, nf_ref, rw_ref, rb_ref,
               x1_ref, hn_ref, ti_ref, tg_ref):
    is_sample = pl.program_id(0) == pl.num_programs(0) - 1

    def mixed(p_ref, s_ref):
        return jnp.where(is_sample, s_ref[...], p_ref[...]).astype(BF16)

    mix = jnp.concatenate([mixed(oap_ref, oas_ref), mixed(obp_ref, obs_ref), mixed(ocp_ref, ocs_ref)], axis=-1)
    x1 = x_ref[...] + jnp.dot(mix, w_ref[...], preferred_element_type=F32)
    x1_ref[...] = x1
    hn = _rms(x1, nf_ref[...])
    bits = lax.bitcast_convert_type(hn.astype(BF16).astype(F32), jnp.uint32)
    hn_ref[...] = (bits[:, :PACK_W] >> 16) | (bits[:, PACK_W:] & jnp.uint32(0xFFFF0000))
    hn_hi = hn.astype(BF16)
    hn_lo = (hn - hn_hi.astype(F32)).astype(BF16)
    rw = rw_ref[...]
    rw_hi = rw.astype(BF16)
    rw_lo = (rw - rw_hi.astype(F32)).astype(BF16)
    logits = (jnp.dot(hn_hi, rw_hi, preferred_element_type=F32) + jnp.dot(hn_hi, rw_lo, preferred_element_type=F32)
              + jnp.dot(hn_lo, rw_hi, preferred_element_type=F32) + rb_ref[...])
    tm = logits.shape[0]
    col = lax.broadcasted_iota(jnp.int32, (tm, N_EXPERTS), 1).astype(F32)
    lane = lax.broadcasted_iota(jnp.int32, (tm, ROUTE_LANES), 1)
    work = logits
    ti = jnp.zeros((tm, ROUTE_LANES), jnp.int32)
    tv = jnp.zeros((tm, ROUTE_LANES), F32)
    vals = []
    for kk in range(TOP_K):
        m = jnp.max(work, axis=-1, keepdims=True)
        idx = jnp.min(jnp.where(work == m, col, float(N_EXPERTS)), axis=-1, keepdims=True)
        work = jnp.where(col == idx, -jnp.inf, work)
        vals.append(m)
        ti = jnp.where(lane == kk, idx.astype(jnp.int32), ti)
    es = [jnp.exp(m - vals[0]) for m in vals]
    tot = functools.reduce(lambda a, b: a + b, es)
    for kk in range(TOP_K):
        tv = jnp.where(lane == kk, es[kk] / tot, tv)
    ti_ref[...] = ti
    tg_ref[...] = tv


def post_mixer(prompt_mix, sample_mix, x, w_out, nf, rw, rb):
    tm = POST_TM
    assert tm == N_S
    d = D_MODEL
    last_p = N_P // tm - 1

    def rows(width):
        return pl.BlockSpec((tm, width), lambda i: (i, 0))

    def prows(width):
        return pl.BlockSpec((tm, width), lambda i: (jnp.minimum(i, last_p), 0))

    def full(shape):
        return pl.BlockSpec(shape, lambda i: (0,) * len(shape))

    widths = (HA_WIDTH, HB_WIDTH, HC_WIDTH)
    return pl.pallas_call(
        _post_body,
        grid=(N_ALL // tm,),
        in_specs=[prows(wd) for wd in widths] + [full((tm, wd)) for wd in widths]
                 + [rows(d), full((d, d)), full((1, d)), full((d, N_EXPERTS)), full((1, N_EXPERTS))],
        out_specs=[rows(d), rows(PACK_W), rows(ROUTE_LANES), rows(ROUTE_LANES)],
        out_shape=[jax.ShapeDtypeStruct((N_ALL, d), F32), jax.ShapeDtypeStruct((N_ALL, PACK_W), jnp.uint32),
                   jax.ShapeDtypeStruct((N_ALL, ROUTE_LANES), jnp.int32),
                   jax.ShapeDtypeStruct((N_ALL, ROUTE_LANES), F32)],
        compiler_params=_cparams(("parallel",)),
        name="post_mixer",
    )(*prompt_mix, *sample_mix, x, w_out, _row(nf), rw, _row(rb))


def _unpack_rows(words):
    lo = lax.bitcast_convert_type(words << 16, F32).astype(BF16)
    hi = lax.bitcast_convert_type(words & jnp.uint32(0xFFFF0000), F32).astype(BF16)
    return lo, hi


def _gmm1_body(te_ref, nv_ref, x_ref, wg_ref, wl_ref, bg_ref, bl_ref, o_ref):
    del te_ref

    @pl.when(pl.program_id(1) < nv_ref[0])
    def _():
        lo, hi = _unpack_rows(x_ref[...])

        def proj(w_ref, b_ref):
            return (jnp.dot(lo, w_ref[0, 0, :PACK_W, :].astype(BF16), preferred_element_type=F32)
                    + jnp.dot(hi, w_ref[0, 0, PACK_W:, :].astype(BF16), preferred_element_type=F32) + b_ref[0, 0])

        glu = jnp.minimum(proj(wg_ref, bg_ref), SWIGLU_LIMIT)
        lin = jnp.clip(proj(wl_ref, bl_ref), -SWIGLU_LIMIT, SWIGLU_LIMIT)
        o_ref[...] = (glu * _sigmoid(SWIGLU_ALPHA * glu) * (lin + 1.0)).astype(BF16)

    @pl.when(pl.program_id(1) >= nv_ref[0])
    def _():
        o_ref[...] = jnp.zeros_like(o_ref)


def _gmm2_body(te_ref, nv_ref, a_ref, w_ref, b_ref, o_ref):
    del te_ref

    @pl.when(pl.program_id(1) < nv_ref[0])
    def _():
        o_ref[...] = jnp.dot(a_ref[...], w_ref[0, 0].astype(BF16), preferred_element_type=F32) + b_ref[0, 0]

    @pl.when(pl.program_id(1) >= nv_ref[0])
    def _():
        o_ref[...] = jnp.zeros_like(o_ref)


def _tile_row(t, nv):
    return jnp.minimum(t, nv[0] - 1)


def moe_gmm1(layer, xs, w1, b1, tile_expert, n_valid):
    tm, tn = MOE_TM, MOE_TN1
    nb = D_FF // tn
    d = D_MODEL
    grid_spec = pltpu.PrefetchScalarGridSpec(
        num_scalar_prefetch=2,
        grid=(nb, MOE_TILES),
        in_specs=[
            pl.BlockSpec((tm, PACK_W), lambda n, t, te, nv: (_tile_row(t, nv), 0)),
            pl.BlockSpec((1, 1, d, tn), lambda n, t, te, nv: (layer, te[t], 0, n)),
            pl.BlockSpec((1, 1, d, tn), lambda n, t, te, nv: (layer, te[t], 0, n + nb)),
            pl.BlockSpec((1, 1, 1, tn), lambda n, t, te, nv: (layer, te[t], 0, n)),
            pl.BlockSpec((1, 1, 1, tn), lambda n, t, te, nv: (layer, te[t], 0, n + nb)),
        ],
        out_specs=pl.BlockSpec((tm, tn), lambda n, t, te, nv: (t, n)),
    )
    b1r = b1.reshape(DEPTH, N_EXPERTS, 1, 2 * D_FF)
    return pl.pallas_call(
        _gmm1_body,
        grid_spec=grid_spec,
        out_shape=jax.ShapeDtypeStruct((MOE_ROWS, D_FF), BF16),
        compiler_params=_cparams(("parallel", "arbitrary")),
        name="moe_gmm1",
    )(tile_expert, n_valid, xs, w1, w1, b1r, b1r)


def moe_gmm2(layer, act, w2, b2, tile_expert, n_valid):
    tm, tn = MOE_TM, MOE_TN2
    nb = D_MODEL // tn
    grid_spec = pltpu.PrefetchScalarGridSpec(
        num_scalar_prefetch=2,
        grid=(nb, MOE_TILES),
        in_specs=[
            pl.BlockSpec((tm, D_FF), lambda n, t, te, nv: (_tile_row(t, nv), 0)),
            pl.BlockSpec((1, 1, D_FF, tn), lambda n, t, te, nv: (layer, te[t], 0, n)),
            pl.BlockSpec((1, 1, 1, tn), lambda n, t, te, nv: (layer, te[t], 0, n)),
        ],
        out_specs=pl.BlockSpec((tm, tn), lambda n, t, te, nv: (t, n)),
    )
    return pl.pallas_call(
        _gmm2_body,
        grid_spec=grid_spec,
        out_shape=jax.ShapeDtypeStruct((MOE_ROWS, D_MODEL), F32),
        compiler_params=_cparams(("parallel", "arbitrary")),
        name="moe_gmm2",
    )(tile_expert, n_valid, act, w2, b2.reshape(DEPTH, N_EXPERTS, 1, D_MODEL))


DISP_CHUNK = 256
N_SLOTS = N_ALL * TOP_K


def _dispatch_body(pos_ref, hp_ref, xs_in_ref, xs_ref, sem):
    del xs_in_ref
    s = pl.program_id(0)
    per_chunk = DISP_CHUNK * TOP_K

    def row_copy(j, slot):
        tok = lax.shift_right_logical(j, 2)
        return pltpu.make_async_copy(hp_ref.at[pl.ds(tok, 1)], xs_ref.at[pl.ds(pos_ref[j], 1)], sem.at[slot])

    def chunk(c, start):
        def body(i, carry):
            cp = row_copy(c * per_chunk + i, c % 2)
            cp.start() if start else cp.wait()
            return carry
        lax.fori_loop(0, per_chunk, body, 0, unroll=8)

    chunk(s, True)

    @pl.when(s > 0)
    def _():
        chunk(s - 1, False)

    @pl.when(s == pl.num_programs(0) - 1)
    def _():
        chunk(s, False)


def moe_dispatch(pos, hp):
    assert TOP_K == 4
    any_spec = pl.BlockSpec(memory_space=pl.ANY)
    grid_spec = pltpu.PrefetchScalarGridSpec(
        num_scalar_prefetch=1,
        grid=(N_ALL // DISP_CHUNK,),
        in_specs=[any_spec, any_spec],
        out_specs=any_spec,
        scratch_shapes=[pltpu.SemaphoreType.DMA((2,))],
    )
    xs0 = jnp.zeros((MOE_ROWS, PACK_W), jnp.uint32)
    return pl.pallas_call(
        _dispatch_body,
        grid_spec=grid_spec,
        out_shape=jax.ShapeDtypeStruct((MOE_ROWS, PACK_W), jnp.uint32),
        input_output_aliases={2: 0},
        compiler_params=_cparams(("arbitrary",)),
        name="moe_dispatch",
    )(pos, hp, xs0)


COMB_TB = 128


def _combine_body(final, pos_ref, yo_ref, x1_ref, tg_ref, nf_ref, o_ref, buf, sem):
    s = pl.program_id(0)
    ns = pl.num_programs(0)

    def row_copy(step, i, k, slot):
        j = (step * COMB_TB + i) * TOP_K + k
        return pltpu.make_async_copy(yo_ref.at[pl.ds(pos_ref[j], 1)], buf.at[slot, k, pl.ds(i, 1)], sem.at[slot])

    def start_tile(step, slot):
        def body(i, carry):
            for k in range(TOP_K):
                row_copy(step, i, k, slot).start()
            return carry
        lax.fori_loop(0, COMB_TB, body, 0, unroll=4)

    def wait_tile(step, slot):
        def body(i, carry):
            for k in range(TOP_K):
                row_copy(step, i, k, slot).wait()
            return carry
        lax.fori_loop(0, COMB_TB, body, 0, unroll=4)

    @pl.when(s == 0)
    def _():
        start_tile(0, 0)

    @pl.when(s + 1 < ns)
    def _():
        start_tile(s + 1, (s + 1) % 2)

    slot = s % 2
    wait_tile(s, slot)
    acc = x1_ref[...]
    for k in range(TOP_K):
        acc = acc + tg_ref[:, k:k + 1] * buf[slot, k]
    o_ref[...] = _rms(acc, nf_ref[...]) if final else acc


def moe_combine(pos, yo, x1, tg, nf, final):
    tb = COMB_TB
    grid_spec = pltpu.PrefetchScalarGridSpec(
        num_scalar_prefetch=1,
        grid=(N_ALL // tb,),
        in_specs=[pl.BlockSpec(memory_space=pl.ANY),
                  pl.BlockSpec((tb, D_MODEL), lambda i, pos: (i, 0)),
                  pl.BlockSpec((tb, ROUTE_LANES), lambda i, pos: (i, 0)),
                  pl.BlockSpec((1, D_MODEL), lambda i, pos: (0, 0))],
        out_specs=pl.BlockSpec((tb, D_MODEL), lambda i, pos: (i, 0)),
        scratch_shapes=[pltpu.VMEM((2, TOP_K, tb, D_MODEL), F32), pltpu.SemaphoreType.DMA((2,))],
    )
    return pl.pallas_call(
        functools.partial(_combine_body, final),
        grid_spec=grid_spec,
        out_shape=jax.ShapeDtypeStruct((N_ALL, D_MODEL), F32),
        compiler_params=_cparams(("arbitrary",)),
        name="moe_combine",
    )(pos, yo, x1, tg, _row(nf))


def moe_layout(ti):
    flat_e = ti.reshape(-1)
    oh = (flat_e[:, None] == jnp.arange(N_EXPERTS, dtype=jnp.int32)[None, :]).astype(jnp.int32)
    csum = jnp.cumsum(oh, axis=0)
    counts = csum[-1]
    rank = jnp.sum((csum - oh) * oh, axis=1)
    tiles_e = (counts + MOE_TM - 1) // MOE_TM
    tile_end = jnp.cumsum(tiles_e)
    row_off = (tile_end - tiles_e) * MOE_TM
    pos = row_off[flat_e] + rank
    tile_expert = jnp.minimum(
        jnp.searchsorted(tile_end, jnp.arange(MOE_TILES, dtype=jnp.int32), side="right"), N_EXPERTS - 1
    ).astype(jnp.int32)
    n_valid = tile_end[-1:].astype(jnp.int32)
    return pos.astype(jnp.int32), tile_expert, n_valid


def _block_diag(w):
    eye = jnp.eye(HB_BLOCKS, dtype=w.dtype)
    return jnp.einsum("hij,hg->higj", w, eye).reshape(HB_WIDTH, HB_WIDTH)


def kernel(x_prompt, x_sample, mem_prompt, state_hgrn, state_lru, state_conv, cache_mem_k, cache_mem_v,
           norm_mix, w_in, hgrn_lb, hgrn_onorm, conv_w, conv_b, lru_wa, lru_ba, lru_wx, lru_bx, lru_L,
           norm_lru, norm_xattn, norm_mem, w_mem_k, w_mem_v, w_out, norm_ffn, router_w, router_b,
           moe_w1, moe_b1, moe_w2, moe_b2, norm_final):
    x = jnp.concatenate([x_prompt.reshape(N_P, D_MODEL), x_sample.reshape(N_S, D_MODEL)], axis=0)
    mem = mem_prompt.reshape(BATCH * N_MEM, D_MODEL)
    cache_k = cache_mem_k.reshape(DEPTH, DEC_BATCH, N_MEM * HC_HEADS, HC_DH)
    cache_v = cache_mem_v.reshape(DEPTH, DEC_BATCH, N_MEM * HC_HEADS, HC_DH)
    conv_s = state_conv.reshape(DEPTH, DEC_BATCH, (CONV_W - 1) * HB_WIDTH)
    hgrn_s_out = jnp.zeros(state_hgrn.shape, F32)

    p_hgrn, p_lru, p_conv, p_mk, p_mv, s_lru, s_conv = [], [], [], [], [], [], []
    for l in range(DEPTH):
        w_mem = jnp.concatenate([w_mem_k[l], w_mem_v[l]], axis=1).astype(BF16)
        memkv = norm_matmul(mem, norm_mem[l], w_mem, tm=BATCH * N_MEM, tn=512)
        p_mk.append(memkv[:, :HC_WIDTH].reshape(BATCH, N_MEM, HC_HEADS, HC_DH))
        p_mv.append(memkv[:, HC_WIDTH:].reshape(BATCH, N_MEM, HC_HEADS, HC_DH))

        u = norm_matmul(x, norm_mix[l], w_in[l].astype(BF16), tm=N_ALL // 8, tn=512)

        oa_p, hg_p = hgrn_prompt(l, u, hgrn_lb, hgrn_onorm[l])
        oa_s, hgrn_s_out = hgrn_sample(l, u, hgrn_lb, hgrn_onorm[l], state_hgrn, hgrn_s_out)

        wa = _block_diag(lru_wa[l]).astype(BF16)
        wx = _block_diag(lru_wx[l]).astype(BF16)
        lru_args = (conv_w[l], conv_b[l], wa, wx, lru_ba[l], lru_bx[l], lru_L[l], norm_lru[l])
        ob_p, hl_p, cs_p = lru_prompt(u, *lru_args)
        ob_s, hl_s, cs_s = lru_sample(l, u, state_lru, conv_s, *lru_args)

        oc_p = xattn_prompt(u, memkv, norm_xattn[l])
        oc_s = xattn_sample(l, u, cache_k, cache_v, norm_xattn[l])

        x1, hp, ti, tg = post_mixer((oa_p, ob_p, oc_p), (oa_s, ob_s, oc_s), x, w_out[l].astype(BF16),
                                    norm_ffn[l], router_w[l], router_b[l])
        pos, tile_expert, n_valid = moe_layout(ti[:, :TOP_K])
        xs = moe_dispatch(pos, hp)
        act = moe_gmm1(l, xs, moe_w1, moe_b1, tile_expert, n_valid)
        yo = moe_gmm2(l, act, moe_w2, moe_b2, tile_expert, n_valid)
        x = moe_combine(pos, yo, x1, tg, norm_final, final=(l == DEPTH - 1))

        p_hgrn.append(hg_p)
        p_lru.append(hl_p.reshape(BATCH, HB_WIDTH))
        p_conv.append(cs_p)
        s_lru.append(hl_s)
        s_conv.append(cs_s.reshape(DEC_BATCH, CONV_W - 1, HB_WIDTH))

    y_prompt = x[:N_P].reshape(BATCH, SEQ, D_MODEL)
    y_sample = x[N_P:].reshape(DEC_BATCH, DEC_SEQ, D_MODEL)
    return (y_prompt, y_sample, jnp.stack(p_hgrn), jnp.stack(p_lru), jnp.stack(p_conv),
            jnp.stack(p_mk), jnp.stack(p_mv), hgrn_s_out, jnp.stack(s_lru), jnp.stack(s_conv))
```

```python
import functools
import math

import jax
import jax.numpy as jnp
from jax import lax
from jax.experimental import pallas as pl
from jax.experimental.pallas import tpu as pltpu

F32 = jnp.float32
BF16 = jnp.bfloat16

D_MODEL = 2048
BATCH = 4
SEQ = 2048
DEPTH = 2
DEC_BATCH = 128
DEC_SEQ = 4
HA_DK = 128
HA_WIDTH = D_MODEL // 2
HA_HEADS = HA_WIDTH // HA_DK
HA_DV = HA_WIDTH // HA_HEADS
HB_WIDTH = D_MODEL // 4
HB_BLOCKS = 8
HB_BW = HB_WIDTH // HB_BLOCKS
CONV_W = 4
LRU_C = 8.0
HC_HEADS = 4
HC_WIDTH = D_MODEL - HA_WIDTH - HB_WIDTH
HC_DH = HC_WIDTH // HC_HEADS
N_MEM = 256
IN_COLS = 4 * HA_WIDTH + 2 * HB_WIDTH + HC_WIDTH
N_EXPERTS = 32
TOP_K = 4
D_FF = D_MODEL
SWIGLU_LIMIT = 7.0
SWIGLU_ALPHA = 1.702
EPS = 1e-6

N_P = BATCH * SEQ
N_S = DEC_BATCH * DEC_SEQ
N_ALL = N_P + N_S

COL_QA, COL_FA, COL_IA, COL_GA = 0, 1, 2, 3
COL_XB, COL_GB, COL_QC = 8, 9, 10

HGRN_CHUNK = 64
HGRN_SUB = 16
LRU_TB = 256
XATTN_TQ = 512
MOE_TM = 256
MOE_TN1 = 512
MOE_TN2 = 1024
MOE_TILES = (N_ALL * TOP_K) // MOE_TM + N_EXPERTS
MOE_ROWS = MOE_TILES * MOE_TM
VMEM_LIMIT = 56 * 1024 * 1024

NT_DIMS = (((1,), (1,)), ((), ()))
TN_DIMS = (((0,), (0,)), ((), ()))


def _cparams(sem):
    return pltpu.CompilerParams(dimension_semantics=sem, vmem_limit_bytes=VMEM_LIMIT)


def _rms(x, g):
    return x * lax.rsqrt(jnp.mean(x * x, axis=-1, keepdims=True) + EPS) * g


def _sigmoid(x):
    return jax.nn.sigmoid(x)


def _silu(x):
    return x * jax.nn.sigmoid(x)


def _norm_matmul_body(x_ref, g_ref, w_ref, o_ref, xn_ref):
    @pl.when(pl.program_id(1) == 0)
    def _():
        xn_ref[...] = _rms(x_ref[...], g_ref[...]).astype(BF16)

    o_ref[...] = jnp.dot(xn_ref[...], w_ref[...], preferred_element_type=F32)


def norm_matmul(x, g, w, tm, tn):
    n, d = x.shape
    nc = w.shape[1]
    return pl.pallas_call(
        _norm_matmul_body,
        grid=(n // tm, nc // tn),
        in_specs=[
            pl.BlockSpec((tm, d), lambda i, j: (i, 0)),
            pl.BlockSpec((1, d), lambda i, j: (0, 0)),
            pl.BlockSpec((d, tn), lambda i, j: (0, j)),
        ],
        out_specs=pl.BlockSpec((tm, tn), lambda i, j: (i, j)),
        out_shape=jax.ShapeDtypeStruct((n, nc), F32),
        scratch_shapes=[pltpu.VMEM((tm, d), BF16)],
        compiler_params=_cparams(("parallel", "arbitrary")),
        name="norm_matmul",
    )(x, g.reshape(1, d), w)


def _hgrn_lower_bound(lbp, layer):
    rows = [lbp[r:r + 1, :] for r in range(DEPTH)]
    m = functools.reduce(jnp.maximum, rows)
    es = [jnp.exp(r - m) for r in rows]
    tot = functools.reduce(lambda a, b: a + b, es)
    lb = jnp.zeros_like(m)
    for r in range(1, layer + 1):
        lb = lb + es[r] / tot
    return lb


def _hgrn_gates(z, qa, lb):
    f = lb + (1.0 - lb) * _sigmoid(z)
    k = (1.0 - lb) * _sigmoid(-z)
    q = _silu(qa)
    return f, k, q


def _hgrn_prompt_body(layer, qa_ref, fa_ref, ia_ref, ga_ref, lbp_ref, on_ref, oa_ref, st_ref, s_scr):
    c = pl.program_id(1)
    C, SUB = HGRN_CHUNK, HGRN_SUB
    nsub = C // SUB

    @pl.when(c == 0)
    def _():
        s_scr[...] = jnp.zeros_like(s_scr)

    lb = _hgrn_lower_bound(lbp_ref[...], layer)
    f, k, q = _hgrn_gates(fa_ref[...], qa_ref[...], lb)
    g = jnp.log(f)
    v = ia_ref[...]
    gate = _silu(ga_ref[...])

    row = lax.broadcasted_iota(jnp.int32, (C, C), 0)
    col = lax.broadcasted_iota(jnp.int32, (C, C), 1)
    tri = (row >= col).astype(F32)
    b_all = jnp.dot(tri, g, preferred_element_type=F32, precision=lax.Precision.HIGHEST)

    row_sub = lax.broadcasted_iota(jnp.int32, (SUB, HA_DK), 0)
    row_c = lax.broadcasted_iota(jnp.int32, (C, HA_DK), 0)
    lane_c = lax.broadcasted_iota(jnp.int32, (SUB, C), 1)
    neg_inf = jnp.float32(-jnp.inf)

    states = [s_scr[h] for h in range(HA_HEADS)]
    new_states, outs = [], []
    for h in range(HA_HEADS):
        sl = slice(h * HA_DK, (h + 1) * HA_DK)
        bh, qh, kh, vh = b_all[:, sl], q[:, sl], k[:, sl], v[:, sl]
        kh16 = kh.astype(BF16)
        vh16 = vh.astype(BF16)
        b_end = bh[C - 1:C, :]
        st = states[h]
        o = lax.dot_general((qh * jnp.exp(bh)).astype(BF16), st.astype(BF16), NT_DIMS,
                            preferred_element_type=F32)
        kdec = (kh * jnp.exp(b_end - bh)).astype(BF16)
        upd = lax.dot_general(vh16, kdec, TN_DIMS, preferred_element_type=F32)
        new_states.append(st * jnp.exp(b_end) + upd)

        a_rows = []
        for i in range(nsub):
            bi = bh[i * SUB:(i + 1) * SUB, :]
            qi = qh[i * SUB:(i + 1) * SUB, :]
            parts = []
            for s in range(SUB):
                dec = jnp.exp(jnp.where(row_sub >= s, bi - bi[s:s + 1, :], neg_inf))
                parts.append(qi * dec)
            q_all = jnp.concatenate(parts, axis=0).astype(BF16)
            m = lax.dot_general(q_all, kh16, NT_DIMS, preferred_element_type=F32)
            a_i = jnp.zeros((SUB, C), F32)
            for s in range(SUB):
                a_i = a_i + jnp.where(lane_c == i * SUB + s, m[s * SUB:(s + 1) * SUB, :], 0.0)
            if i > 0:
                b_prev = bh[i * SUB - 1:i * SUB, :]
                q_i = (qi * jnp.exp(bi - b_prev)).astype(BF16)
                k_i = (kh * jnp.exp(jnp.where(row_c < i * SUB, b_prev - bh, neg_inf))).astype(BF16)
                a_i = a_i + lax.dot_general(q_i, k_i, NT_DIMS, preferred_element_type=F32)
            a_rows.append(a_i)
        att = jnp.concatenate(a_rows, axis=0).astype(BF16)
        o = o + jnp.dot(att, vh16, preferred_element_type=F32)
        outs.append(_rms(o, on_ref[...]) * gate[:, sl])

    for h in range(HA_HEADS):
        oa_ref[:, h * HA_DK:(h + 1) * HA_DK] = outs[h]
        s_scr[h] = new_states[h]

    @pl.when(c == pl.num_programs(1) - 1)
    def _():
        for h in range(HA_HEADS):
            st_ref[0, h] = new_states[h].T


def hgrn_prompt(layer, u, lbp, onorm):
    nchunk = SEQ // HGRN_CHUNK

    def spec(colblk):
        return pl.BlockSpec((HGRN_CHUNK, HA_WIDTH), lambda b, c: (b * nchunk + c, colblk))

    return pl.pallas_call(
        functools.partial(_hgrn_prompt_body, layer),
        grid=(BATCH, nchunk),
        in_specs=[spec(COL_QA), spec(COL_FA), spec(COL_IA), spec(COL_GA),
                  pl.BlockSpec((DEPTH, HA_WIDTH), lambda b, c: (0, 0)),
                  pl.BlockSpec((1, HA_DV), lambda b, c: (0, 0))],
        out_specs=[pl.BlockSpec((HGRN_CHUNK, HA_WIDTH), lambda b, c: (b * nchunk + c, 0)),
                   pl.BlockSpec((1, HA_HEADS, HA_DK, HA_DV), lambda b, c: (b, 0, 0, 0))],
        out_shape=[jax.ShapeDtypeStruct((N_P, HA_WIDTH), F32),
                   jax.ShapeDtypeStruct((BATCH, HA_HEADS, HA_DK, HA_DV), F32)],
        scratch_shapes=[pltpu.VMEM((HA_HEADS, HA_DV, HA_DK), F32)],
        compiler_params=_cparams(("parallel", "arbitrary")),
        name="hgrn_prompt",
    )(u, u, u, u, lbp, onorm.reshape(1, HA_DV))


HGRN_SB = 2


def _hgrn_sample_body(layer, qa_ref, fa_ref, ia_ref, ga_ref, lbp_ref, on_ref, s_ref, so_in_ref, oa_ref, so_ref):
    del so_in_ref
    assert HGRN_SB == 2 and DEC_SEQ >= 3
    R = HGRN_SB * DEC_SEQ
    lb = _hgrn_lower_bound(lbp_ref[...], layer)
    f, k, q = _hgrn_gates(fa_ref[...], qa_ref[...], lb)
    g = jnp.log(f)
    v = ia_ref[...]
    gate = _silu(ga_ref[...])

    ri = lax.broadcasted_iota(jnp.int32, (R, R), 0)
    ci = lax.broadcasted_iota(jnp.int32, (R, R), 1)
    same_elem = (ri >= DEC_SEQ) == (ci >= DEC_SEQ)
    tri = (same_elem & (ri >= ci)).astype(F32)
    b = jnp.dot(tri, g, preferred_element_type=F32, precision=lax.Precision.HIGHEST)
    row_w = lax.broadcasted_iota(jnp.int32, b.shape, 0)
    b_end = jnp.where(row_w < DEC_SEQ, b[DEC_SEQ - 1:DEC_SEQ, :], b[R - 1:R, :])
    qe = q * jnp.exp(b)
    kdec = k * jnp.exp(b_end - b)
    e_end = jnp.exp(b_end)

    row = lax.broadcasted_iota(jnp.int32, (R, HA_DK), 0)
    elem = (row >= DEC_SEQ).astype(jnp.int32)
    trow = row - elem * DEC_SEQ
    lane_r = lax.broadcasted_iota(jnp.int32, (R, R), 1)
    neg_inf = jnp.float32(-jnp.inf)

    states = [[s_ref[0, bb, h] for h in range(HA_HEADS)] for bb in range(HGRN_SB)]
    new_states = [[None] * HA_HEADS for _ in range(HGRN_SB)]
    outs = []
    for h in range(HA_HEADS):
        sl = slice(h * HA_DK, (h + 1) * HA_DK)
        bh, qh, kh, vh = b[:, sl], q[:, sl], k[:, sl], v[:, sl]
        vh16 = vh.astype(BF16)
        parts = []
        for s in range(R):
            valid = (elem == s // DEC_SEQ) & (row >= s)
            parts.append(qh * jnp.exp(jnp.where(valid, bh - bh[s:s + 1, :], neg_inf)))
        q_all = jnp.concatenate(parts, axis=0).astype(BF16)
        m = lax.dot_general(q_all, kh.astype(BF16), NT_DIMS, preferred_element_type=F32)
        att = jnp.zeros((R, R), F32)
        for s in range(R):
            att = att + jnp.where(lane_r == s, m[s * R:(s + 1) * R, :], 0.0)
        o = jnp.dot(att.astype(BF16), vh16, preferred_element_type=F32)

        qe16 = qe[:, sl].astype(BF16)
        for bb in range(HGRN_SB):
            s0 = states[bb][h]
            o = o + jnp.where(elem == bb, jnp.dot(qe16, s0.astype(BF16), preferred_element_type=F32), 0.0)
            e = e_end[bb * DEC_SEQ:bb * DEC_SEQ + 1, sl]
            e1 = e.astype(BF16).astype(F32)
            e2 = (e - e1).astype(BF16).astype(F32)
            e3 = e - e1 - e2
            eparts = jnp.where(trow == 0, e1, jnp.where(trow == 1, e2, jnp.where(trow == 2, e3, 0.0)))
            own = elem == bb
            lhs = jnp.where(own, kdec[:, sl], eparts).astype(BF16)
            ones = jnp.where(own | (trow > 2), 0.0, 1.0)
            rhs = jnp.concatenate([ones, jnp.where(own, vh, 0.0)], axis=-1).astype(BF16)
            res = lax.dot_general(lhs, rhs, TN_DIMS, preferred_element_type=F32)
            new_states[bb][h] = res[:, :HA_DV] * s0 + res[:, HA_DV:]
        outs.append(_rms(o, on_ref[...]) * gate[:, sl])

    for h in range(HA_HEADS):
        oa_ref[:, h * HA_DK:(h + 1) * HA_DK] = outs[h]
        for bb in range(HGRN_SB):
            so_ref[0, bb, h] = new_states[bb][h]


def hgrn_sample(layer, u, lbp, onorm, state, so_buf):
    rows8 = HGRN_SB * DEC_SEQ
    row0 = N_P // rows8

    def spec(colblk):
        return pl.BlockSpec((rows8, HA_WIDTH), lambda i: (row0 + i, colblk))

    st_spec = pl.BlockSpec((1, HGRN_SB, HA_HEADS, HA_DK, HA_DV), lambda i: (layer, i, 0, 0, 0))
    return pl.pallas_call(
        functools.partial(_hgrn_sample_body, layer),
        grid=(DEC_BATCH // HGRN_SB,),
        in_specs=[spec(COL_QA), spec(COL_FA), spec(COL_IA), spec(COL_GA),
                  pl.BlockSpec((DEPTH, HA_WIDTH), lambda i: (0, 0)),
                  pl.BlockSpec((1, HA_DV), lambda i: (0, 0)),
                  st_spec, pl.BlockSpec(memory_space=pl.ANY)],
        out_specs=[pl.BlockSpec((rows8, HA_WIDTH), lambda i: (i, 0)), st_spec],
        out_shape=[jax.ShapeDtypeStruct((N_S, HA_WIDTH), F32), jax.ShapeDtypeStruct(state.shape, F32)],
        input_output_aliases={7: 1},
        compiler_params=_cparams(("parallel",)),
        name="hgrn_sample",
    )(u, u, u, u, lbp, onorm.reshape(1, HA_DV), state, so_buf)


def _softplus(x):
    return jnp.maximum(x, 0.0) + jnp.log1p(jnp.exp(-jnp.abs(x)))


def _lru_gates(xc, wa_ref, wx_ref, ba, bx, sp):
    xc16 = xc.astype(BF16)
    r = _sigmoid(jnp.dot(xc16, wa_ref[...], preferred_element_type=F32) + ba)
    i = _sigmoid(jnp.dot(xc16, wx_ref[...], preferred_element_type=F32) + bx)
    log_a = -LRU_C * r * sp
    a = jnp.exp(log_a)
    th = jnp.tanh(log_a)
    mult = jnp.sqrt(2.0 * th / (th - 1.0))
    return a, mult * i * xc


def _lru_prompt_body(xb_ref, gb_ref, cw_ref, cb_ref, wa_ref, wx_ref, ba_ref, bx_ref, l_ref, nl_ref,
                     ob_ref, hl_ref, cs_ref, xe_scr, h_scr):
    t = pl.program_id(1)
    tb = LRU_TB

    @pl.when(t == 0)
    def _():
        xe_scr[0:8, :] = jnp.zeros((8, HB_WIDTH), F32)
        h_scr[...] = jnp.zeros_like(h_scr)

    xb = xb_ref[...]
    xe_scr[8:8 + tb, :] = xb
    cw = cw_ref[...]
    xc = cb_ref[...] + cw[3:4, :] * xb
    for j in range(1, CONV_W):
        xc = xc + cw[3 - j:4 - j, :] * xe_scr[pl.ds(8 - j, tb), :]
    xe_scr[0:8, :] = xb[tb - 8:tb, :]

    sp = _softplus(-l_ref[...])
    a, bt = _lru_gates(xc, wa_ref, wx_ref, ba_ref[...], bx_ref[...], sp)
    row = lax.broadcasted_iota(jnp.int32, (tb, HB_WIDTH), 0)
    sh = 1
    while sh < tb:
        keep = row >= sh
        a_sh = jnp.where(keep, pltpu.roll(a, sh, 0), 1.0)
        b_sh = jnp.where(keep, pltpu.roll(bt, sh, 0), 0.0)
        bt = a * b_sh + bt
        a = a * a_sh
        sh *= 2
    hcur = bt + a * h_scr[0:1, :]
    h_last = hcur[tb - 1:tb, :]
    h_scr[...] = jnp.broadcast_to(h_last, h_scr.shape)
    y = hcur * jax.nn.gelu(gb_ref[...])
    ob_ref[...] = _rms(y, nl_ref[...])
    hl_ref[0] = h_last
    cs_ref[0] = xb[tb - (CONV_W - 1):tb, :]


def _row(p):
    return p.reshape(1, -1)


def lru_prompt(u, cw, cb, wa, wx, ba, bx, lam, nl):
    nt = SEQ // LRU_TB
    w = HB_WIDTH

    def uspec(colblk):
        return pl.BlockSpec((LRU_TB, w), lambda b, t: (b * nt + t, colblk))

    def full(shape):
        return pl.BlockSpec(shape, lambda b, t: (0,) * len(shape))

    return pl.pallas_call(
        _lru_prompt_body,
        grid=(BATCH, nt),
        in_specs=[uspec(COL_XB), uspec(COL_GB), full((CONV_W, w)), full((1, w)), full((w, w)), full((w, w)),
                  full((1, w)), full((1, w)), full((1, w)), full((1, w))],
        out_specs=[pl.BlockSpec((LRU_TB, w), lambda b, t: (b * nt + t, 0)),
                   pl.BlockSpec((1, 1, w), lambda b, t: (b, 0, 0)),
                   pl.BlockSpec((1, CONV_W - 1, w), lambda b, t: (b, 0, 0))],
        out_shape=[jax.ShapeDtypeStruct((N_P, w), F32),
                   jax.ShapeDtypeStruct((BATCH, 1, w), F32),
                   jax.ShapeDtypeStruct((BATCH, CONV_W - 1, w), F32)],
        scratch_shapes=[pltpu.VMEM((LRU_TB + 8, w), F32), pltpu.VMEM((8, w), F32)],
        compiler_params=_cparams(("parallel", "arbitrary")),
        name="lru_prompt",
    )(u, u, cw, _row(cb), wa, wx, _row(ba), _row(bx), _row(lam), _row(nl))


def _lru_sample_body(xb_ref, gb_ref, h0_ref, cv_ref, cw_ref, cb_ref, wa_ref, wx_ref, ba_ref, bx_ref, l_ref, nl_ref,
                     ob_ref, hl_ref, cs_ref, x_scr, g_scr, y_scr):
    w = HB_WIDTH
    nb = DEC_BATCH
    nchunk = w // 128
    cw = cw_ref[...]
    for c in range(nchunk):
        x_scr[c] = xb_ref[:, c * 128:(c + 1) * 128]
        g_scr[c] = gb_ref[:, c * 128:(c + 1) * 128]

    def time_rows(scr, t):
        return jnp.concatenate([scr[c, pl.ds(t, nb, stride=DEC_SEQ), :] for c in range(nchunk)], axis=-1)

    xs = [cv_ref[0, :, j * w:(j + 1) * w] for j in range(CONV_W - 1)]
    xs += [time_rows(x_scr, t) for t in range(DEC_SEQ)]
    sp = _softplus(-l_ref[...])
    hcur = h0_ref[0]
    for t in range(DEC_SEQ):
        xc = cb_ref[...]
        for j in range(CONV_W):
            xc = xc + cw[j:j + 1, :] * xs[t + j]
        a, bt = _lru_gates(xc, wa_ref, wx_ref, ba_ref[...], bx_ref[...], sp)
        hcur = a * hcur + bt
        y = _rms(hcur * jax.nn.gelu(time_rows(g_scr, t)), nl_ref[...])
        for c in range(nchunk):
            y_scr[c, pl.ds(t, nb, stride=DEC_SEQ), :] = y[:, c * 128:(c + 1) * 128]
    for c in range(nchunk):
        ob_ref[:, c * 128:(c + 1) * 128] = y_scr[c]
    hl_ref[...] = hcur
    for j in range(CONV_W - 1):
        cs_ref[:, j * w:(j + 1) * w] = xs[DEC_SEQ + j]


def lru_sample(layer, u, h0, conv, cw, cb, wa, wx, ba, bx, lam, nl):
    w = HB_WIDTH
    row0 = N_P // N_S

    def full(shape):
        return pl.BlockSpec(shape, lambda i: (0,) * len(shape))

    return pl.pallas_call(
        _lru_sample_body,
        grid=(1,),
        in_specs=[pl.BlockSpec((N_S, w), lambda i: (row0, COL_XB)),
                  pl.BlockSpec((N_S, w), lambda i: (row0, COL_GB)),
                  pl.BlockSpec((1, DEC_BATCH, w), lambda i: (layer, 0, 0)),
                  pl.BlockSpec((1, DEC_BATCH, (CONV_W - 1) * w), lambda i: (layer, 0, 0)),
                  full((CONV_W, w)), full((1, w)), full((w, w)), full((w, w)),
                  full((1, w)), full((1, w)), full((1, w)), full((1, w))],
        out_specs=[full((N_S, w)), full((DEC_BATCH, w)), full((DEC_BATCH, (CONV_W - 1) * w))],
        out_shape=[jax.ShapeDtypeStruct((N_S, w), F32),
                   jax.ShapeDtypeStruct((DEC_BATCH, w), F32),
                   jax.ShapeDtypeStruct((DEC_BATCH, (CONV_W - 1) * w), F32)],
        scratch_shapes=[pltpu.VMEM((w // 128, N_S, 128), F32)] * 3,
        compiler_params=_cparams(("arbitrary",)),
        name="lru_sample",
    )(u, u, h0, conv, cw, _row(cb), wa, wx, _row(ba), _row(bx), _row(lam), _row(nl))


XATTN_SCALE = 1.0 / math.sqrt(HC_DH)


def _xattn_prompt_body(q_ref, k_ref, v_ref, nx_ref, oc_ref):
    q = q_ref[...]
    outs = []
    for h in range(HC_HEADS):
        sl = slice(h * HC_DH, (h + 1) * HC_DH)
        s = lax.dot_general(q[:, sl].astype(BF16), k_ref[0, :, sl].astype(BF16), NT_DIMS,
                            preferred_element_type=F32) * XATTN_SCALE
        p = jnp.exp(s - jnp.max(s, axis=-1, keepdims=True))
        p = p / jnp.sum(p, axis=-1, keepdims=True)
        outs.append(jnp.dot(p.astype(BF16), v_ref[0, :, sl].astype(BF16), preferred_element_type=F32))
    oc_ref[...] = _rms(jnp.concatenate(outs, axis=-1), nx_ref[...])


def xattn_prompt(u, memkv, nx):
    nq = SEQ // XATTN_TQ
    w = HC_WIDTH
    kv = memkv.reshape(BATCH, N_MEM, 2 * w)
    return pl.pallas_call(
        _xattn_prompt_body,
        grid=(BATCH, nq),
        in_specs=[pl.BlockSpec((XATTN_TQ, w), lambda b, t: (b * nq + t, COL_QC)),
                  pl.BlockSpec((1, N_MEM, w), lambda b, t: (b, 0, 0)),
                  pl.BlockSpec((1, N_MEM, w), lambda b, t: (b, 0, 1)),
                  pl.BlockSpec((1, w), lambda b, t: (0, 0))],
        out_specs=pl.BlockSpec((XATTN_TQ, w), lambda b, t: (b * nq + t, 0)),
        out_shape=jax.ShapeDtypeStruct((N_P, w), F32),
        compiler_params=_cparams(("parallel", "parallel")),
        name="xattn_prompt",
    )(u, kv, kv, _row(nx))


XATTN_SB = 8


def _xattn_sample_body(q_ref, k_ref, v_ref, nx_ref, oc_ref, o_scr):
    q = q_ref[...]
    row8 = lax.broadcasted_iota(jnp.int32, (8, HC_DH), 0)
    for pair in range(XATTN_SB // 2):
        for h in range(HC_HEADS):
            sl = slice(h * HC_DH, (h + 1) * HC_DH)
            q8 = q[pair * 8:(pair + 1) * 8, sl].astype(BF16)
            o8 = []
            for e in range(2):
                bb = pair * 2 + e
                kh = k_ref[0, bb, pl.ds(h, N_MEM, stride=HC_HEADS), :].astype(BF16)
                vh = v_ref[0, bb, pl.ds(h, N_MEM, stride=HC_HEADS), :].astype(BF16)
                st = lax.dot_general(kh, q8, NT_DIMS, preferred_element_type=F32) * XATTN_SCALE
                pt = jnp.exp(st - jnp.max(st, axis=0, keepdims=True))
                pt = pt / jnp.sum(pt, axis=0, keepdims=True)
                o8.append(lax.dot_general(pt.astype(BF16), vh, TN_DIMS, preferred_element_type=F32))
            o_scr[pair * 8:(pair + 1) * 8, sl] = jnp.where(row8 < DEC_SEQ, o8[0], o8[1])
    oc_ref[...] = _rms(o_scr[...], nx_ref[...])


def xattn_sample(layer, u, cache_k, cache_v, nx):
    w = HC_WIDTH
    rows = XATTN_SB * DEC_SEQ
    row0 = N_P // rows
    kv_spec = pl.BlockSpec((1, XATTN_SB, N_MEM * HC_HEADS, HC_DH), lambda i: (layer, i, 0, 0))
    return pl.pallas_call(
        _xattn_sample_body,
        grid=(DEC_BATCH // XATTN_SB,),
        in_specs=[pl.BlockSpec((rows, w), lambda i: (row0 + i, COL_QC)), kv_spec, kv_spec,
                  pl.BlockSpec((1, w), lambda i: (0, 0))],
        out_specs=pl.BlockSpec((rows, w), lambda i: (i, 0)),
        out_shape=jax.ShapeDtypeStruct((N_S, w), F32),
        scratch_shapes=[pltpu.VMEM((rows, w), F32)],
        compiler_params=_cparams(("parallel",)),
        name="xattn_sample",
    )(u, cache_k, cache_v, _row(nx))


POST_TM = 512
ROUTE_LANES = 128
PACK_W = D_MODEL // 2


def _post_body(oap_ref, obp_ref, ocp_ref, oas_ref, obs_ref, ocs_ref, x_ref, w_ref, nf_ref, rw_ref, rb_ref,
               x1_ref, hn_ref, ti_ref, tg_ref):
    is_sample = pl.program_id(0) == pl.num_programs(0) - 1

    def mixed(p_ref, s_ref):
        return jnp.where(is_sample, s_ref[...], p_ref[...]).astype(BF16)

    mix = jnp.concatenate([mixed(oap_ref, oas_ref), mixed(obp_ref, obs_ref), mixed(ocp_ref, ocs_ref)], axis=-1)
    x1 = x_ref[...] + jnp.dot(mix, w_ref[...], preferred_element_type=F32)
    x1_ref[...] = x1
    hn = _rms(x1, nf_ref[...])
    bits = lax.bitcast_convert_type(hn.astype(BF16).astype(F32), jnp.uint32)
    hn_ref[...] = (bits[:, :PACK_W] >> 16) | (bits[:, PACK_W:] & jnp.uint32(0xFFFF0000))
    hn_hi = hn.astype(BF16)
    hn_lo = (hn - hn_hi.astype(F32)).astype(BF16)
    rw = rw_ref[...]
    rw_hi = rw.astype(BF16)
    rw_lo = (rw - rw_hi.astype(F32)).astype(BF16)
    logits = (jnp.dot(hn_hi, rw_hi, preferred_element_type=F32) + jnp.dot(hn_hi, rw_lo, preferred_element_type=F32)
              + jnp.dot(hn_lo, rw_hi, preferred_element_type=F32) + rb_ref[...])
    tm = logits.shape[0]
    col = lax.broadcasted_iota(jnp.int32, (tm, N_EXPERTS), 1).astype(F32)
    lane = lax.broadcasted_iota(jnp.int32, (tm, ROUTE_LANES), 1)
    work = logits
    ti = jnp.zeros((tm, ROUTE_LANES), jnp.int32)
    tv = jnp.zeros((tm, ROUTE_LANES), F32)
    vals = []
    for kk in range(TOP_K):
        m = jnp.max(work, axis=-1, keepdims=True)
        idx = jnp.min(jnp.where(work == m, col, float(N_EXPERTS)), axis=-1, keepdims=True)
        work = jnp.where(col == idx, -jnp.inf, work)
        vals.append(m)
        ti = jnp.where(lane == kk, idx.astype(jnp.int32), ti)
    es = [jnp.exp(m - vals[0]) for m in vals]
    tot = functools.reduce(lambda a, b: a + b, es)
    for kk in range(TOP_K):
        tv = jnp.where(lane == kk, es[kk] / tot, tv)
    ti_ref[...] = ti
    tg_ref[...] = tv


def post_mixer(prompt_mix, sample_mix, x, w_out, nf, rw, rb):
    tm = POST_TM
    assert tm == N_S
    d = D_MODEL
    last_p = N_P // tm - 1

    def rows(width):
        return pl.BlockSpec((tm, width), lambda i: (i, 0))

    def prows(width):
        return pl.BlockSpec((tm, width), lambda i: (jnp.minimum(i, last_p), 0))

    def full(shape):
        return pl.BlockSpec(shape, lambda i: (0,) * len(shape))

    widths = (HA_WIDTH, HB_WIDTH, HC_WIDTH)
    return pl.pallas_call(
        _post_body,
        grid=(N_ALL // tm,),
        in_specs=[prows(wd) for wd in widths] + [full((tm, wd)) for wd in widths]
                 + [rows(d), full((d, d)), full((1, d)), full((d, N_EXPERTS)), full((1, N_EXPERTS))],
        out_specs=[rows(d), rows(PACK_W), rows(ROUTE_LANES), rows(ROUTE_LANES)],
        out_shape=[jax.ShapeDtypeStruct((N_ALL, d), F32), jax.ShapeDtypeStruct((N_ALL, PACK_W), jnp.uint32),
                   jax.ShapeDtypeStruct((N_ALL, ROUTE_LANES), jnp.int32),
                   jax.ShapeDtypeStruct((N_ALL, ROUTE_LANES), F32)],
        compiler_params=_cparams(("parallel",)),
        name="post_mixer",
    )(*prompt_mix, *sample_mix, x, w_out, _row(nf), rw, _row(rb))


def _unpack_rows(words):
    lo = lax.bitcast_convert_type(words << 16, F32).astype(BF16)
    hi = lax.bitcast_convert_type(words & jnp.uint32(0xFFFF0000), F32).astype(BF16)
    return lo, hi


def _expert_tiles(ts_ref, nt_ref, src_ref, dst_ref, tn, in_buf, out_buf, sem_in, sem_out, compute):
    n, e = pl.program_id(0), pl.program_id(1)
    step = n * pl.num_programs(1) + e
    last_step = pl.num_programs(0) * pl.num_programs(1) - 1
    t0, nt = ts_ref[e], nt_ref[e]
    tm = MOE_TM

    def in_copy(tile, slot):
        return pltpu.make_async_copy(src_ref.at[pl.ds(tile * tm, tm)], in_buf.at[slot], sem_in.at[slot])

    def out_copy(tile, slot):
        return pltpu.make_async_copy(out_buf.at[slot], dst_ref.at[pl.ds(tile * tm, tm), pl.ds(n * tn, tn)],
                                     sem_out.at[slot])

    @pl.when((step == 0) & (nt > 0))
    def _():
        in_copy(t0, 0).start()

    def tile_body(i, carry):
        slot = i % 2
        in_copy(t0 + i, slot).wait()

        @pl.when(i + 1 < nt)
        def _():
            in_copy(t0 + i + 1, 1 - slot).start()

        res = compute(in_buf[slot])

        @pl.when(i >= 2)
        def _():
            out_copy(t0 + i - 2, slot).wait()

        out_buf[slot] = res
        out_copy(t0 + i, slot).start()
        return carry

    lax.fori_loop(0, nt, tile_body, 0)

    @pl.when(nt >= 2)
    def _():
        out_copy(t0 + nt - 2, nt % 2).wait()

    @pl.when(nt >= 1)
    def _():
        out_copy(t0 + nt - 1, (nt - 1) % 2).wait()

    e_next = jnp.where(e + 1 < pl.num_programs(1), e + 1, 0)

    @pl.when((step < last_step) & (nt_ref[e_next] > 0))
    def _():
        in_copy(ts_ref[e_next], 0).start()

    @pl.when(e == pl.num_programs(1) - 1)
    def _():
        out_buf[0] = jnp.zeros(out_buf.shape[1:], out_buf.dtype)

        def fill(tile, carry):
            out_copy(tile, 0).start()
            out_copy(tile, 0).wait()
            return carry

        lax.fori_loop(t0 + nt, MOE_TILES, fill, 0)


def _gmm1_body(ts_ref, nt_ref, xs_ref, wg_ref, wl_ref, bg_ref, bl_ref, act_ref,
               in_buf, out_buf, w_scr, sem_in, sem_out):
    tn = MOE_TN1

    @pl.when(nt_ref[pl.program_id(1)] > 0)
    def _():
        w_scr[:, :tn] = wg_ref[0, 0].astype(BF16)
        w_scr[:, tn:] = wl_ref[0, 0].astype(BF16)

    def compute(words):
        lo, hi = _unpack_rows(words)
        u = (jnp.dot(lo, w_scr[:PACK_W, :], preferred_element_type=F32)
             + jnp.dot(hi, w_scr[PACK_W:, :], preferred_element_type=F32))
        glu = jnp.minimum(u[:, :tn] + bg_ref[0, 0], SWIGLU_LIMIT)
        lin = jnp.clip(u[:, tn:] + bl_ref[0, 0], -SWIGLU_LIMIT, SWIGLU_LIMIT)
        return (glu * _sigmoid(SWIGLU_ALPHA * glu) * (lin + 1.0)).astype(BF16)

    _expert_tiles(ts_ref, nt_ref, xs_ref, act_ref, tn, in_buf, out_buf, sem_in, sem_out, compute)


def _gmm2_body(ts_ref, nt_ref, act_ref, w_ref, b_ref, yo_ref, in_buf, out_buf, w_scr, sem_in, sem_out):
    @pl.when(nt_ref[pl.program_id(1)] > 0)
    def _():
        w_scr[...] = w_ref[0, 0].astype(BF16)

    def compute(a):
        return jnp.dot(a, w_scr[...], preferred_element_type=F32) + b_ref[0, 0]

    _expert_tiles(ts_ref, nt_ref, act_ref, yo_ref, MOE_TN2, in_buf, out_buf, sem_in, sem_out, compute)


def _gmm_scratch(in_shape, in_dtype, tn, out_dtype):
    return [pltpu.VMEM((2,) + in_shape, in_dtype), pltpu.VMEM((2, MOE_TM, tn), out_dtype),
            pltpu.SemaphoreType.DMA((2,)), pltpu.SemaphoreType.DMA((2,))]


def moe_gmm1(layer, xs, w1, b1, tile_start, tile_count):
    tn = MOE_TN1
    nb = D_FF // tn
    d = D_MODEL
    any_spec = pl.BlockSpec(memory_space=pl.ANY)
    in_buf, out_buf, sem_in, sem_out = _gmm_scratch((MOE_TM, PACK_W), jnp.uint32, tn, BF16)
    grid_spec = pltpu.PrefetchScalarGridSpec(
        num_scalar_prefetch=2,
        grid=(nb, N_EXPERTS),
        in_specs=[
            any_spec,
            pl.BlockSpec((1, 1, d, tn), lambda n, e, ts, nt: (layer, e, 0, n)),
            pl.BlockSpec((1, 1, d, tn), lambda n, e, ts, nt: (layer, e, 0, n + nb)),
            pl.BlockSpec((1, 1, 1, tn), lambda n, e, ts, nt: (layer, e, 0, n)),
            pl.BlockSpec((1, 1, 1, tn), lambda n, e, ts, nt: (layer, e, 0, n + nb)),
        ],
        out_specs=any_spec,
        scratch_shapes=[in_buf, out_buf, pltpu.VMEM((d, 2 * tn), BF16), sem_in, sem_out],
    )
    b1r = b1.reshape(DEPTH, N_EXPERTS, 1, 2 * D_FF)
    return pl.pallas_call(
        _gmm1_body,
        grid_spec=grid_spec,
        out_shape=jax.ShapeDtypeStruct((MOE_ROWS, D_FF), BF16),
        compiler_params=_cparams(("arbitrary", "arbitrary")),
        name="moe_gmm1",
    )(tile_start, tile_count, xs, w1, w1, b1r, b1r)


def moe_gmm2(layer, act, w2, b2, tile_start, tile_count):
    tn = MOE_TN2
    nb = D_MODEL // tn
    any_spec = pl.BlockSpec(memory_space=pl.ANY)
    in_buf, out_buf, sem_in, sem_out = _gmm_scratch((MOE_TM, D_FF), BF16, tn, F32)
    grid_spec = pltpu.PrefetchScalarGridSpec(
        num_scalar_prefetch=2,
        grid=(nb, N_EXPERTS),
        in_specs=[
            any_spec,
            pl.BlockSpec((1, 1, D_FF, tn), lambda n, e, ts, nt: (layer, e, 0, n)),
            pl.BlockSpec((1, 1, 1, tn), lambda n, e, ts, nt: (layer, e, 0, n)),
        ],
        out_specs=any_spec,
        scratch_shapes=[in_buf, out_buf, pltpu.VMEM((D_FF, tn), BF16), sem_in, sem_out],
    )
    return pl.pallas_call(
        _gmm2_body,
        grid_spec=grid_spec,
        out_shape=jax.ShapeDtypeStruct((MOE_ROWS, D_MODEL), F32),
        compiler_params=_cparams(("arbitrary", "arbitrary")),
        name="moe_gmm2",
    )(tile_start, tile_count, act, w2, b2.reshape(DEPTH, N_EXPERTS, 1, D_MODEL))


DISP_CHUNK = 256
N_SLOTS = N_ALL * TOP_K


def _dispatch_body(pos_ref, hp_ref, xs_in_ref, xs_ref, sem):
    del xs_in_ref
    base = pl.program_id(0) * (DISP_CHUNK * TOP_K)

    def row_copy(i, k):
        return pltpu.make_async_copy(hp_ref.at[pl.ds(i, 1)], xs_ref.at[pl.ds(pos_ref[base + i * TOP_K + k], 1)], sem)

    def all_rows(start):
        def body(i, carry):
            for k in range(TOP_K):
                cp = row_copy(i, k)
                cp.start() if start else cp.wait()
            return carry
        lax.fori_loop(0, DISP_CHUNK, body, 0, unroll=4)

    all_rows(True)
    all_rows(False)


def moe_dispatch(pos, hp):
    any_spec = pl.BlockSpec(memory_space=pl.ANY)
    grid_spec = pltpu.PrefetchScalarGridSpec(
        num_scalar_prefetch=1,
        grid=(N_ALL // DISP_CHUNK,),
        in_specs=[pl.BlockSpec((DISP_CHUNK, PACK_W), lambda i, pos: (i, 0)), any_spec],
        out_specs=any_spec,
        scratch_shapes=[pltpu.SemaphoreType.DMA(())],
    )
    xs0 = jnp.zeros((MOE_ROWS, PACK_W), jnp.uint32)
    return pl.pallas_call(
        _dispatch_body,
        grid_spec=grid_spec,
        out_shape=jax.ShapeDtypeStruct((MOE_ROWS, PACK_W), jnp.uint32),
        input_output_aliases={2: 0},
        compiler_params=_cparams(("arbitrary",)),
        name="moe_dispatch",
    )(pos, hp, xs0)


COMB_TB = 128


def _combine_body(final, pos_ref, yo_ref, x1_ref, tg_ref, nf_ref, o_ref, buf, sem):
    s = pl.program_id(0)
    ns = pl.num_programs(0)

    def row_copy(step, i, k, slot):
        j = (step * COMB_TB + i) * TOP_K + k
        return pltpu.make_async_copy(yo_ref.at[pl.ds(pos_ref[j], 1)], buf.at[slot, k, pl.ds(i, 1)], sem.at[slot])

    def start_tile(step, slot):
        def body(i, carry):
            for k in range(TOP_K):
                row_copy(step, i, k, slot).start()
            return carry
        lax.fori_loop(0, COMB_TB, body, 0, unroll=4)

    def wait_tile(step, slot):
        def body(i, carry):
            for k in range(TOP_K):
                row_copy(step, i, k, slot).wait()
            return carry
        lax.fori_loop(0, COMB_TB, body, 0, unroll=4)

    @pl.when(s == 0)
    def _():
        start_tile(0, 0)

    @pl.when(s + 1 < ns)
    def _():
        start_tile(s + 1, (s + 1) % 2)

    slot = s % 2
    wait_tile(s, slot)
    acc = x1_ref[...]
    for k in range(TOP_K):
        acc = acc + tg_ref[:, k:k + 1] * buf[slot, k]
    o_ref[...] = _rms(acc, nf_ref[...]) if final else acc


def moe_combine(pos, yo, x1, tg, nf, final):
    tb = COMB_TB
    grid_spec = pltpu.PrefetchScalarGridSpec(
        num_scalar_prefetch=1,
        grid=(N_ALL // tb,),
        in_specs=[pl.BlockSpec(memory_space=pl.ANY),
                  pl.BlockSpec((tb, D_MODEL), lambda i, pos: (i, 0)),
                  pl.BlockSpec((tb, ROUTE_LANES), lambda i, pos: (i, 0)),
                  pl.BlockSpec((1, D_MODEL), lambda i, pos: (0, 0))],
        out_specs=pl.BlockSpec((tb, D_MODEL), lambda i, pos: (i, 0)),
        scratch_shapes=[pltpu.VMEM((2, TOP_K, tb, D_MODEL), F32), pltpu.SemaphoreType.DMA((2,))],
    )
    return pl.pallas_call(
        functools.partial(_combine_body, final),
        grid_spec=grid_spec,
        out_shape=jax.ShapeDtypeStruct((N_ALL, D_MODEL), F32),
        compiler_params=_cparams(("arbitrary",)),
        name="moe_combine",
    )(pos, yo, x1, tg, _row(nf))


def moe_layout(ti):
    flat_e = ti.reshape(-1)
    oh = (flat_e[:, None] == jnp.arange(N_EXPERTS, dtype=jnp.int32)[None, :]).astype(jnp.int32)
    csum = jnp.cumsum(oh, axis=0)
    counts = csum[-1]
    rank = jnp.sum((csum - oh) * oh, axis=1)
    tile_count = (counts + MOE_TM - 1) // MOE_TM
    tile_start = jnp.cumsum(tile_count) - tile_count
    pos = (tile_start * MOE_TM)[flat_e] + rank
    return pos.astype(jnp.int32), tile_start.astype(jnp.int32), tile_count.astype(jnp.int32)


def _block_diag(w):
    eye = jnp.eye(HB_BLOCKS, dtype=w.dtype)
    return jnp.einsum("hij,hg->higj", w, eye).reshape(HB_WIDTH, HB_WIDTH)


def kernel(x_prompt, x_sample, mem_prompt, state_hgrn, state_lru, state_conv, cache_mem_k, cache_mem_v,
           norm_mix, w_in, hgrn_lb, hgrn_onorm, conv_w, conv_b, lru_wa, lru_ba, lru_wx, lru_bx, lru_L,
           norm_lru, norm_xattn, norm_mem, w_mem_k, w_mem_v, w_out, norm_ffn, router_w, router_b,
           moe_w1, moe_b1, moe_w2, moe_b2, norm_final):
    x = jnp.concatenate([x_prompt.reshape(N_P, D_MODEL), x_sample.reshape(N_S, D_MODEL)], axis=0)
    mem = mem_prompt.reshape(BATCH * N_MEM, D_MODEL)
    cache_k = cache_mem_k.reshape(DEPTH, DEC_BATCH, N_MEM * HC_HEADS, HC_DH)
    cache_v = cache_mem_v.reshape(DEPTH, DEC_BATCH, N_MEM * HC_HEADS, HC_DH)
    conv_s = state_conv.reshape(DEPTH, DEC_BATCH, (CONV_W - 1) * HB_WIDTH)
    hgrn_s_out = jnp.zeros(state_hgrn.shape, F32)

    p_hgrn, p_lru, p_conv, p_mk, p_mv, s_lru, s_conv = [], [], [], [], [], [], []
    for l in range(DEPTH):
        w_mem = jnp.concatenate([w_mem_k[l], w_mem_v[l]], axis=1).astype(BF16)
        memkv = norm_matmul(mem, norm_mem[l], w_mem, tm=BATCH * N_MEM, tn=512)
        p_mk.append(memkv[:, :HC_WIDTH].reshape(BATCH, N_MEM, HC_HEADS, HC_DH))
        p_mv.append(memkv[:, HC_WIDTH:].reshape(BATCH, N_MEM, HC_HEADS, HC_DH))

        u = norm_matmul(x, norm_mix[l], w_in[l].astype(BF16), tm=N_ALL // 8, tn=512)

        oa_p, hg_p = hgrn_prompt(l, u, hgrn_lb, hgrn_onorm[l])
        oa_s, hgrn_s_out = hgrn_sample(l, u, hgrn_lb, hgrn_onorm[l], state_hgrn, hgrn_s_out)

        wa = _block_diag(lru_wa[l]).astype(BF16)
        wx = _block_diag(lru_wx[l]).astype(BF16)
        lru_args = (conv_w[l], conv_b[l], wa, wx, lru_ba[l], lru_bx[l], lru_L[l], norm_lru[l])
        ob_p, hl_p, cs_p = lru_prompt(u, *lru_args)
        ob_s, hl_s, cs_s = lru_sample(l, u, state_lru, conv_s, *lru_args)

        oc_p = xattn_prompt(u, memkv, norm_xattn[l])
        oc_s = xattn_sample(l, u, cache_k, cache_v, norm_xattn[l])

        x1, hp, ti, tg = post_mixer((oa_p, ob_p, oc_p), (oa_s, ob_s, oc_s), x, w_out[l].astype(BF16),
                                    norm_ffn[l], router_w[l], router_b[l])
        pos, tile_start, tile_count = moe_layout(ti[:, :TOP_K])
        xs = moe_dispatch(pos, hp)
        act = moe_gmm1(l, xs, moe_w1, moe_b1, tile_start, tile_count)
        yo = moe_gmm2(l, act, moe_w2, moe_b2, tile_start, tile_count)
        x = moe_combine(pos, yo, x1, tg, norm_final, final=(l == DEPTH - 1))

        p_hgrn.append(hg_p)
        p_lru.append(hl_p.reshape(BATCH, HB_WIDTH))
        p_conv.append(cs_p)
        s_lru.append(hl_s)
        s_conv.append(cs_s.reshape(DEC_BATCH, CONV_W - 1, HB_WIDTH))

    y_prompt = x[:N_P].reshape(BATCH, SEQ, D_MODEL)
    y_sample = x[N_P:].reshape(DEC_BATCH, DEC_SEQ, D_MODEL)
    return (y_prompt, y_sample, jnp.stack(p_hgrn), jnp.stack(p_lru), jnp.stack(p_conv),
            jnp.stack(p_mk), jnp.stack(p_mv), hgrn_s_out, jnp.stack(s_lru), jnp.stack(s_conv))
```

```python
import functools
import math

import jax
import jax.numpy as jnp
from jax import lax
from jax.experimental import pallas as pl
from jax.experimental.pallas import tpu as pltpu

F32 = jnp.float32
BF16 = jnp.bfloat16

D_MODEL = 2048
BATCH = 4
SEQ = 2048
DEPTH = 2
DEC_BATCH = 128
DEC_SEQ = 4
HA_DK = 128
HA_WIDTH = D_MODEL // 2
HA_HEADS = HA_WIDTH // HA_DK
HA_DV = HA_WIDTH // HA_HEADS
HB_WIDTH = D_MODEL // 4
HB_BLOCKS = 8
HB_BW = HB_WIDTH // HB_BLOCKS
CONV_W = 4
LRU_C = 8.0
HC_HEADS = 4
HC_WIDTH = D_MODEL - HA_WIDTH - HB_WIDTH
HC_DH = HC_WIDTH // HC_HEADS
N_MEM = 256
IN_COLS = 4 * HA_WIDTH + 2 * HB_WIDTH + HC_WIDTH
N_EXPERTS = 32
TOP_K = 4
D_FF = D_MODEL
SWIGLU_LIMIT = 7.0
SWIGLU_ALPHA = 1.702
EPS = 1e-6

N_P = BATCH * SEQ
N_S = DEC_BATCH * DEC_SEQ
N_ALL = N_P + N_S

COL_QA, COL_FA, COL_IA, COL_GA = 0, 1, 2, 3
COL_XB, COL_GB, COL_QC = 8, 9, 10

HGRN_CHUNK = 64
HGRN_SUB = 16
LRU_TB = 256
XATTN_TQ = 512
MOE_TM = 256
MOE_LOOKAHEAD = 3
MOE_TN1 = 512
MOE_TN2 = 1024
MOE_TILES = (N_ALL * TOP_K) // MOE_TM + N_EXPERTS
MOE_ROWS = MOE_TILES * MOE_TM
VMEM_LIMIT = 56 * 1024 * 1024

NT_DIMS = (((1,), (1,)), ((), ()))
TN_DIMS = (((0,), (0,)), ((), ()))


def _cparams(sem):
    return pltpu.CompilerParams(dimension_semantics=sem, vmem_limit_bytes=VMEM_LIMIT)


def _rms(x, g):
    return x * lax.rsqrt(jnp.mean(x * x, axis=-1, keepdims=True) + EPS) * g


def _sigmoid(x):
    return jax.nn.sigmoid(x)


def _silu(x):
    return x * jax.nn.sigmoid(x)


def _norm_matmul_body(x_ref, g_ref, w_ref, o_ref, xn_ref):
    @pl.when(pl.program_id(1) == 0)
    def _():
        xn_ref[...] = _rms(x_ref[...], g_ref[...]).astype(BF16)

    o_ref[...] = jnp.dot(xn_ref[...], w_ref[...], preferred_element_type=F32)


def norm_matmul(x, g, w, tm, tn):
    n, d = x.shape
    nc = w.shape[1]
    return pl.pallas_call(
        _norm_matmul_body,
        grid=(n // tm, nc // tn),
        in_specs=[
            pl.BlockSpec((tm, d), lambda i, j: (i, 0)),
            pl.BlockSpec((1, d), lambda i, j: (0, 0)),
            pl.BlockSpec((d, tn), lambda i, j: (0, j)),
        ],
        out_specs=pl.BlockSpec((tm, tn), lambda i, j: (i, j)),
        out_shape=jax.ShapeDtypeStruct((n, nc), F32),
        scratch_shapes=[pltpu.VMEM((tm, d), BF16)],
        compiler_params=_cparams(("parallel", "arbitrary")),
        name="norm_matmul",
    )(x, g.reshape(1, d), w)


def _hgrn_lower_bound(lbp, layer):
    rows = [lbp[r:r + 1, :] for r in range(DEPTH)]
    m = functools.reduce(jnp.maximum, rows)
    es = [jnp.exp(r - m) for r in rows]
    tot = functools.reduce(lambda a, b: a + b, es)
    lb = jnp.zeros_like(m)
    for r in range(1, layer + 1):
        lb = lb + es[r] / tot
    return lb


def _hgrn_gates(z, qa, lb):
    f = lb + (1.0 - lb) * _sigmoid(z)
    k = (1.0 - lb) * _sigmoid(-z)
    q = _silu(qa)
    return f, k, q


def _hgrn_prompt_body(layer, qa_ref, fa_ref, ia_ref, ga_ref, lbp_ref, on_ref, oa_ref, st_ref, s_scr):
    c = pl.program_id(1)
    C, SUB = HGRN_CHUNK, HGRN_SUB
    nsub = C // SUB

    @pl.when(c == 0)
    def _():
        s_scr[...] = jnp.zeros_like(s_scr)

    lb = _hgrn_lower_bound(lbp_ref[...], layer)
    f, k, q = _hgrn_gates(fa_ref[...], qa_ref[...], lb)
    g = jnp.log(f)
    v = ia_ref[...]
    gate = _silu(ga_ref[...])

    row = lax.broadcasted_iota(jnp.int32, (C, C), 0)
    col = lax.broadcasted_iota(jnp.int32, (C, C), 1)
    tri = (row >= col).astype(F32)
    b_all = jnp.dot(tri, g, preferred_element_type=F32, precision=lax.Precision.HIGHEST)

    row_sub = lax.broadcasted_iota(jnp.int32, (SUB, HA_DK), 0)
    row_c = lax.broadcasted_iota(jnp.int32, (C, HA_DK), 0)
    lane_c = lax.broadcasted_iota(jnp.int32, (SUB, C), 1)
    neg_inf = jnp.float32(-jnp.inf)

    states = [s_scr[h] for h in range(HA_HEADS)]
    new_states, outs = [], []
    for h in range(HA_HEADS):
        sl = slice(h * HA_DK, (h + 1) * HA_DK)
        bh, qh, kh, vh = b_all[:, sl], q[:, sl], k[:, sl], v[:, sl]
        kh16 = kh.astype(BF16)
        vh16 = vh.astype(BF16)
        b_end = bh[C - 1:C, :]
        st = states[h]
        o = lax.dot_general((qh * jnp.exp(bh)).astype(BF16), st.astype(BF16), NT_DIMS,
                            preferred_element_type=F32)
        kdec = (kh * jnp.exp(b_end - bh)).astype(BF16)
        upd = lax.dot_general(vh16, kdec, TN_DIMS, preferred_element_type=F32)
        new_states.append(st * jnp.exp(b_end) + upd)

        a_rows = []
        for i in range(nsub):
            bi = bh[i * SUB:(i + 1) * SUB, :]
            qi = qh[i * SUB:(i + 1) * SUB, :]
            parts = []
            for s in range(SUB):
                dec = jnp.exp(jnp.where(row_sub >= s, bi - bi[s:s + 1, :], neg_inf))
                parts.append(qi * dec)
            q_all = jnp.concatenate(parts, axis=0).astype(BF16)
            m = lax.dot_general(q_all, kh16, NT_DIMS, preferred_element_type=F32)
            a_i = jnp.zeros((SUB, C), F32)
            for s in range(SUB):
                a_i = a_i + jnp.where(lane_c == i * SUB + s, m[s * SUB:(s + 1) * SUB, :], 0.0)
            if i > 0:
                b_prev = bh[i * SUB - 1:i * SUB, :]
                q_i = (qi * jnp.exp(bi - b_prev)).astype(BF16)
                k_i = (kh * jnp.exp(jnp.where(row_c < i * SUB, b_prev - bh, neg_inf))).astype(BF16)
                a_i = a_i + lax.dot_general(q_i, k_i, NT_DIMS, preferred_element_type=F32)
            a_rows.append(a_i)
        att = jnp.concatenate(a_rows, axis=0).astype(BF16)
        o = o + jnp.dot(att, vh16, preferred_element_type=F32)
        outs.append(_rms(o, on_ref[...]) * gate[:, sl])

    for h in range(HA_HEADS):
        oa_ref[:, h * HA_DK:(h + 1) * HA_DK] = outs[h]
        s_scr[h] = new_states[h]

    @pl.when(c == pl.num_programs(1) - 1)
    def _():
        for h in range(HA_HEADS):
            st_ref[0, h] = new_states[h].T


def hgrn_prompt(layer, u, lbp, onorm):
    nchunk = SEQ // HGRN_CHUNK

    def spec(colblk):
        return pl.BlockSpec((HGRN_CHUNK, HA_WIDTH), lambda b, c: (b * nchunk + c, colblk))

    return pl.pallas_call(
        functools.partial(_hgrn_prompt_body, layer),
        grid=(BATCH, nchunk),
        in_specs=[spec(COL_QA), spec(COL_FA), spec(COL_IA), spec(COL_GA),
                  pl.BlockSpec((DEPTH, HA_WIDTH), lambda b, c: (0, 0)),
                  pl.BlockSpec((1, HA_DV), lambda b, c: (0, 0))],
        out_specs=[pl.BlockSpec((HGRN_CHUNK, HA_WIDTH), lambda b, c: (b * nchunk + c, 0)),
                   pl.BlockSpec((1, HA_HEADS, HA_DK, HA_DV), lambda b, c: (b, 0, 0, 0))],
        out_shape=[jax.ShapeDtypeStruct((N_P, HA_WIDTH), F32),
                   jax.ShapeDtypeStruct((BATCH, HA_HEADS, HA_DK, HA_DV), F32)],
        scratch_shapes=[pltpu.VMEM((HA_HEADS, HA_DV, HA_DK), F32)],
        compiler_params=_cparams(("parallel", "arbitrary")),
        name="hgrn_prompt",
    )(u, u, u, u, lbp, onorm.reshape(1, HA_DV))


HGRN_SB = 2


def _hgrn_sample_body(layer, qa_ref, fa_ref, ia_ref, ga_ref, lbp_ref, on_ref, s_ref, so_in_ref, oa_ref, so_ref):
    del so_in_ref
    assert HGRN_SB == 2 and DEC_SEQ >= 3
    R = HGRN_SB * DEC_SEQ
    lb = _hgrn_lower_bound(lbp_ref[...], layer)
    f, k, q = _hgrn_gates(fa_ref[...], qa_ref[...], lb)
    g = jnp.log(f)
    v = ia_ref[...]
    gate = _silu(ga_ref[...])

    ri = lax.broadcasted_iota(jnp.int32, (R, R), 0)
    ci = lax.broadcasted_iota(jnp.int32, (R, R), 1)
    same_elem = (ri >= DEC_SEQ) == (ci >= DEC_SEQ)
    tri = (same_elem & (ri >= ci)).astype(F32)
    b = jnp.dot(tri, g, preferred_element_type=F32, precision=lax.Precision.HIGHEST)
    row_w = lax.broadcasted_iota(jnp.int32, b.shape, 0)
    b_end = jnp.where(row_w < DEC_SEQ, b[DEC_SEQ - 1:DEC_SEQ, :], b[R - 1:R, :])
    qe = q * jnp.exp(b)
    kdec = k * jnp.exp(b_end - b)
    e_end = jnp.exp(b_end)

    row = lax.broadcasted_iota(jnp.int32, (R, HA_DK), 0)
    elem = (row >= DEC_SEQ).astype(jnp.int32)
    trow = row - elem * DEC_SEQ
    lane_r = lax.broadcasted_iota(jnp.int32, (R, R), 1)
    neg_inf = jnp.float32(-jnp.inf)

    states = [[s_ref[0, bb, h] for h in range(HA_HEADS)] for bb in range(HGRN_SB)]
    new_states = [[None] * HA_HEADS for _ in range(HGRN_SB)]
    outs = []
    for h in range(HA_HEADS):
        sl = slice(h * HA_DK, (h + 1) * HA_DK)
        bh, qh, kh, vh = b[:, sl], q[:, sl], k[:, sl], v[:, sl]
        vh16 = vh.astype(BF16)
        parts = []
        for s in range(R):
            valid = (elem == s // DEC_SEQ) & (row >= s)
            parts.append(qh * jnp.exp(jnp.where(valid, bh - bh[s:s + 1, :], neg_inf)))
        q_all = jnp.concatenate(parts, axis=0).astype(BF16)
        m = lax.dot_general(q_all, kh.astype(BF16), NT_DIMS, preferred_element_type=F32)
        att = jnp.zeros((R, R), F32)
        for s in range(R):
            att = att + jnp.where(lane_r == s, m[s * R:(s + 1) * R, :], 0.0)
        o = jnp.dot(att.astype(BF16), vh16, preferred_element_type=F32)

        qe16 = qe[:, sl].astype(BF16)
        for bb in range(HGRN_SB):
            s0 = states[bb][h]
            o = o + jnp.where(elem == bb, jnp.dot(qe16, s0.astype(BF16), preferred_element_type=F32), 0.0)
            e = e_end[bb * DEC_SEQ:bb * DEC_SEQ + 1, sl]
            e1 = e.astype(BF16).astype(F32)
            e2 = (e - e1).astype(BF16).astype(F32)
            e3 = e - e1 - e2
            eparts = jnp.where(trow == 0, e1, jnp.where(trow == 1, e2, jnp.where(trow == 2, e3, 0.0)))
            own = elem == bb
            lhs = jnp.where(own, kdec[:, sl], eparts).astype(BF16)
            ones = jnp.where(own | (trow > 2), 0.0, 1.0)
            rhs = jnp.concatenate([ones, jnp.where(own, vh, 0.0)], axis=-1).astype(BF16)
            res = lax.dot_general(lhs, rhs, TN_DIMS, preferred_element_type=F32)
            new_states[bb][h] = res[:, :HA_DV] * s0 + res[:, HA_DV:]
        outs.append(_rms(o, on_ref[...]) * gate[:, sl])

    for h in range(HA_HEADS):
        oa_ref[:, h * HA_DK:(h + 1) * HA_DK] = outs[h]
        for bb in range(HGRN_SB):
            so_ref[0, bb, h] = new_states[bb][h]


def hgrn_sample(layer, u, lbp, onorm, state, so_buf):
    rows8 = HGRN_SB * DEC_SEQ
    row0 = N_P // rows8

    def spec(colblk):
        return pl.BlockSpec((rows8, HA_WIDTH), lambda i: (row0 + i, colblk))

    st_spec = pl.BlockSpec((1, HGRN_SB, HA_HEADS, HA_DK, HA_DV), lambda i: (layer, i, 0, 0, 0))
    return pl.pallas_call(
        functools.partial(_hgrn_sample_body, layer),
        grid=(DEC_BATCH // HGRN_SB,),
        in_specs=[spec(COL_QA), spec(COL_FA), spec(COL_IA), spec(COL_GA),
                  pl.BlockSpec((DEPTH, HA_WIDTH), lambda i: (0, 0)),
                  pl.BlockSpec((1, HA_DV), lambda i: (0, 0)),
                  st_spec, pl.BlockSpec(memory_space=pl.ANY)],
        out_specs=[pl.BlockSpec((rows8, HA_WIDTH), lambda i: (i, 0)), st_spec],
        out_shape=[jax.ShapeDtypeStruct((N_S, HA_WIDTH), F32), jax.ShapeDtypeStruct(state.shape, F32)],
        input_output_aliases={7: 1},
        compiler_params=_cparams(("parallel",)),
        name="hgrn_sample",
    )(u, u, u, u, lbp, onorm.reshape(1, HA_DV), state, so_buf)


def _softplus(x):
    return jnp.maximum(x, 0.0) + jnp.log1p(jnp.exp(-jnp.abs(x)))


def _lru_gates(xc, wa_ref, wx_ref, ba, bx, sp):
    xc16 = xc.astype(BF16)
    r = _sigmoid(jnp.dot(xc16, wa_ref[...], preferred_element_type=F32) + ba)
    i = _sigmoid(jnp.dot(xc16, wx_ref[...], preferred_element_type=F32) + bx)
    log_a = -LRU_C * r * sp
    a = jnp.exp(log_a)
    th = jnp.tanh(log_a)
    mult = jnp.sqrt(2.0 * th / (th - 1.0))
    return a, mult * i * xc


def _lru_prompt_body(xb_ref, gb_ref, cw_ref, cb_ref, wa_ref, wx_ref, ba_ref, bx_ref, l_ref, nl_ref,
                     ob_ref, hl_ref, cs_ref, xe_scr, h_scr):
    t = pl.program_id(1)
    tb = LRU_TB

    @pl.when(t == 0)
    def _():
        xe_scr[0:8, :] = jnp.zeros((8, HB_WIDTH), F32)
        h_scr[...] = jnp.zeros_like(h_scr)

    xb = xb_ref[...]
    xe_scr[8:8 + tb, :] = xb
    cw = cw_ref[...]
    xc = cb_ref[...] + cw[3:4, :] * xb
    for j in range(1, CONV_W):
        xc = xc + cw[3 - j:4 - j, :] * xe_scr[pl.ds(8 - j, tb), :]
    xe_scr[0:8, :] = xb[tb - 8:tb, :]

    sp = _softplus(-l_ref[...])
    a, bt = _lru_gates(xc, wa_ref, wx_ref, ba_ref[...], bx_ref[...], sp)
    row = lax.broadcasted_iota(jnp.int32, (tb, HB_WIDTH), 0)
    sh = 1
    while sh < tb:
        keep = row >= sh
        a_sh = jnp.where(keep, pltpu.roll(a, sh, 0), 1.0)
        b_sh = jnp.where(keep, pltpu.roll(bt, sh, 0), 0.0)
        bt = a * b_sh + bt
        a = a * a_sh
        sh *= 2
    hcur = bt + a * h_scr[0:1, :]
    h_last = hcur[tb - 1:tb, :]
    h_scr[...] = jnp.broadcast_to(h_last, h_scr.shape)
    y = hcur * jax.nn.gelu(gb_ref[...])
    ob_ref[...] = _rms(y, nl_ref[...])
    hl_ref[0] = h_last
    cs_ref[0] = xb[tb - (CONV_W - 1):tb, :]


def _row(p):
    return p.reshape(1, -1)


def lru_prompt(u, cw, cb, wa, wx, ba, bx, lam, nl):
    nt = SEQ // LRU_TB
    w = HB_WIDTH

    def uspec(colblk):
        return pl.BlockSpec((LRU_TB, w), lambda b, t: (b * nt + t, colblk))

    def full(shape):
        return pl.BlockSpec(shape, lambda b, t: (0,) * len(shape))

    return pl.pallas_call(
        _lru_prompt_body,
        grid=(BATCH, nt),
        in_specs=[uspec(COL_XB), uspec(COL_GB), full((CONV_W, w)), full((1, w)), full((w, w)), full((w, w)),
                  full((1, w)), full((1, w)), full((1, w)), full((1, w))],
        out_specs=[pl.BlockSpec((LRU_TB, w), lambda b, t: (b * nt + t, 0)),
                   pl.BlockSpec((1, 1, w), lambda b, t: (b, 0, 0)),
                   pl.BlockSpec((1, CONV_W - 1, w), lambda b, t: (b, 0, 0))],
        out_shape=[jax.ShapeDtypeStruct((N_P, w), F32),
                   jax.ShapeDtypeStruct((BATCH, 1, w), F32),
                   jax.ShapeDtypeStruct((BATCH, CONV_W - 1, w), F32)],
        scratch_shapes=[pltpu.VMEM((LRU_TB + 8, w), F32), pltpu.VMEM((8, w), F32)],
        compiler_params=_cparams(("parallel", "arbitrary")),
        name="lru_prompt",
    )(u, u, cw, _row(cb), wa, wx, _row(ba), _row(bx), _row(lam), _row(nl))


def _lru_sample_body(xb_ref, gb_ref, h0_ref, cv_ref, cw_ref, cb_ref, wa_ref, wx_ref, ba_ref, bx_ref, l_ref, nl_ref,
                     ob_ref, hl_ref, cs_ref, x_scr, g_scr, y_scr):
    w = HB_WIDTH
    nb = DEC_BATCH
    nchunk = w // 128
    cw = cw_ref[...]
    for c in range(nchunk):
        x_scr[c] = xb_ref[:, c * 128:(c + 1) * 128]
        g_scr[c] = gb_ref[:, c * 128:(c + 1) * 128]

    def time_rows(scr, t):
        return jnp.concatenate([scr[c, pl.ds(t, nb, stride=DEC_SEQ), :] for c in range(nchunk)], axis=-1)

    xs = [cv_ref[0, :, j * w:(j + 1) * w] for j in range(CONV_W - 1)]
    xs += [time_rows(x_scr, t) for t in range(DEC_SEQ)]
    sp = _softplus(-l_ref[...])
    hcur = h0_ref[0]
    for t in range(DEC_SEQ):
        xc = cb_ref[...]
        for j in range(CONV_W):
            xc = xc + cw[j:j + 1, :] * xs[t + j]
        a, bt = _lru_gates(xc, wa_ref, wx_ref, ba_ref[...], bx_ref[...], sp)
        hcur = a * hcur + bt
        y = _rms(hcur * jax.nn.gelu(time_rows(g_scr, t)), nl_ref[...])
        for c in range(nchunk):
            y_scr[c, pl.ds(t, nb, stride=DEC_SEQ), :] = y[:, c * 128:(c + 1) * 128]
    for c in range(nchunk):
        ob_ref[:, c * 128:(c + 1) * 128] = y_scr[c]
    hl_ref[...] = hcur
    for j in range(CONV_W - 1):
        cs_ref[:, j * w:(j + 1) * w] = xs[DEC_SEQ + j]


def lru_sample(layer, u, h0, conv, cw, cb, wa, wx, ba, bx, lam, nl):
    w = HB_WIDTH
    row0 = N_P // N_S

    def full(shape):
        return pl.BlockSpec(shape, lambda i: (0,) * len(shape))

    return pl.pallas_call(
        _lru_sample_body,
        grid=(1,),
        in_specs=[pl.BlockSpec((N_S, w), lambda i: (row0, COL_XB)),
                  pl.BlockSpec((N_S, w), lambda i: (row0, COL_GB)),
                  pl.BlockSpec((1, DEC_BATCH, w), lambda i: (layer, 0, 0)),
                  pl.BlockSpec((1, DEC_BATCH, (CONV_W - 1) * w), lambda i: (layer, 0, 0)),
                  full((CONV_W, w)), full((1, w)), full((w, w)), full((w, w)),
                  full((1, w)), full((1, w)), full((1, w)), full((1, w))],
        out_specs=[full((N_S, w)), full((DEC_BATCH, w)), full((DEC_BATCH, (CONV_W - 1) * w))],
        out_shape=[jax.ShapeDtypeStruct((N_S, w), F32),
                   jax.ShapeDtypeStruct((DEC_BATCH, w), F32),
                   jax.ShapeDtypeStruct((DEC_BATCH, (CONV_W - 1) * w), F32)],
        scratch_shapes=[pltpu.VMEM((w // 128, N_S, 128), F32)] * 3,
        compiler_params=_cparams(("arbitrary",)),
        name="lru_sample",
    )(u, u, h0, conv, cw, _row(cb), wa, wx, _row(ba), _row(bx), _row(lam), _row(nl))


XATTN_SCALE = 1.0 / math.sqrt(HC_DH)


def _xattn_prompt_body(q_ref, k_ref, v_ref, nx_ref, oc_ref):
    q = q_ref[...]
    outs = []
    for h in range(HC_HEADS):
        sl = slice(h * HC_DH, (h + 1) * HC_DH)
        s = lax.dot_general(q[:, sl].astype(BF16), k_ref[0, :, sl].astype(BF16), NT_DIMS,
                            preferred_element_type=F32) * XATTN_SCALE
        p = jnp.exp(s - jnp.max(s, axis=-1, keepdims=True))
        p = p / jnp.sum(p, axis=-1, keepdims=True)
        outs.append(jnp.dot(p.astype(BF16), v_ref[0, :, sl].astype(BF16), preferred_element_type=F32))
    oc_ref[...] = _rms(jnp.concatenate(outs, axis=-1), nx_ref[...])


def xattn_prompt(u, memkv, nx):
    nq = SEQ // XATTN_TQ
    w = HC_WIDTH
    kv = memkv.reshape(BATCH, N_MEM, 2 * w)
    return pl.pallas_call(
        _xattn_prompt_body,
        grid=(BATCH, nq),
        in_specs=[pl.BlockSpec((XATTN_TQ, w), lambda b, t: (b * nq + t, COL_QC)),
                  pl.BlockSpec((1, N_MEM, w), lambda b, t: (b, 0, 0)),
                  pl.BlockSpec((1, N_MEM, w), lambda b, t: (b, 0, 1)),
                  pl.BlockSpec((1, w), lambda b, t: (0, 0))],
        out_specs=pl.BlockSpec((XATTN_TQ, w), lambda b, t: (b * nq + t, 0)),
        out_shape=jax.ShapeDtypeStruct((N_P, w), F32),
        compiler_params=_cparams(("parallel", "parallel")),
        name="xattn_prompt",
    )(u, kv, kv, _row(nx))


XATTN_SB = 8


def _xattn_sample_body(q_ref, k_ref, v_ref, nx_ref, oc_ref):
    q = q_ref[...]
    row8 = lax.broadcasted_iota(jnp.int32, (8, HC_DH), 0)
    pair_rows = []
    for pair in range(XATTN_SB // 2):
        head_cols = []
        for h in range(HC_HEADS):
            sl = slice(h * HC_DH, (h + 1) * HC_DH)
            q8 = q[pair * 8:(pair + 1) * 8, sl].astype(BF16)
            o8 = []
            for e in range(2):
                bb = pair * 2 + e
                kh = k_ref[0, bb, pl.ds(h, N_MEM, stride=HC_HEADS), :].astype(BF16)
                vh = v_ref[0, bb, pl.ds(h, N_MEM, stride=HC_HEADS), :].astype(BF16)
                st = lax.dot_general(kh, q8, NT_DIMS, preferred_element_type=F32) * XATTN_SCALE
                pt = jnp.exp(st - jnp.max(st, axis=0, keepdims=True))
                pt = pt / jnp.sum(pt, axis=0, keepdims=True)
                o8.append(lax.dot_general(pt.astype(BF16), vh, TN_DIMS, preferred_element_type=F32))
            head_cols.append(jnp.where(row8 < DEC_SEQ, o8[0], o8[1]))
        pair_rows.append(jnp.concatenate(head_cols, axis=-1))
    oc_ref[...] = _rms(jnp.concatenate(pair_rows, axis=0), nx_ref[...])


def xattn_sample(layer, u, cache_k, cache_v, nx):
    w = HC_WIDTH
    rows = XATTN_SB * DEC_SEQ
    row0 = N_P // rows
    kv_spec = pl.BlockSpec((1, XATTN_SB, N_MEM * HC_HEADS, HC_DH), lambda i: (layer, i, 0, 0))
    return pl.pallas_call(
        _xattn_sample_body,
        grid=(DEC_BATCH // XATTN_SB,),
        in_specs=[pl.BlockSpec((rows, w), lambda i: (row0 + i, COL_QC)), kv_spec, kv_spec,
                  pl.BlockSpec((1, w), lambda i: (0, 0))],
        out_specs=pl.BlockSpec((rows, w), lambda i: (i, 0)),
        out_shape=jax.ShapeDtypeStruct((N_S, w), F32),
        compiler_params=_cparams(("parallel",)),
        name="xattn_sample",
    )(u, cache_k, cache_v, _row(nx))


POST_TM = 512
ROUTE_LANES = 128
PACK_W = D_MODEL // 2


def _post_body(oap_ref, obp_ref, ocp_ref, oas_ref, obs_ref, ocs_ref, x_ref, w_ref, nf_ref, rw_ref, rb_ref,
               x1_ref, hn_ref, ti_ref, tg_ref):
    is_sample = pl.program_id(0) == pl.num_programs(0) - 1

    def mixed(p_ref, s_ref):
        return jnp.where(is_sample, s_ref[...], p_ref[...]).astype(BF16)

    mix = jnp.concatenate([mixed(oap_ref, oas_ref), mixed(obp_ref, obs_ref), mixed(ocp_ref, ocs_ref)], axis=-1)
    x1 = x_ref[...] + jnp.dot(mix, w_ref[...], preferred_element_type=F32)
    x1_ref[...] = x1
    hn = _rms(x1, nf_ref[...])
    bits = lax.bitcast_convert_type(hn.astype(BF16).astype(F32), jnp.uint32)
    hn_ref[...] = (bits[:, :PACK_W] >> 16) | (bits[:, PACK_W:] & jnp.uint32(0xFFFF0000))
    hn_hi = hn.astype(BF16)
    hn_lo = (hn - hn_hi.astype(F32)).astype(BF16)
    rw = rw_ref[...]
    rw_hi = rw.astype(BF16)
    rw_lo = (rw - rw_hi.astype(F32)).astype(BF16)
    logits = (jnp.dot(hn_hi, rw_hi, preferred_element_type=F32) + jnp.dot(hn_hi, rw_lo, preferred_element_type=F32)
              + jnp.dot(hn_lo, rw_hi, preferred_element_type=F32) + rb_ref[...])
    tm = logits.shape[0]
    col = lax.broadcasted_iota(jnp.int32, (tm, N_EXPERTS), 1).astype(F32)
    lane = lax.broadcasted_iota(jnp.int32, (tm, ROUTE_LANES), 1)
    work = logits
    ti = jnp.zeros((tm, ROUTE_LANES), jnp.int32)
    tv = jnp.zeros((tm, ROUTE_LANES), F32)
    vals = []
    for kk in range(TOP_K):
        m = jnp.max(work, axis=-1, keepdims=True)
        idx = jnp.min(jnp.where(work == m, col, float(N_EXPERTS)), axis=-1, keepdims=True)
        work = jnp.where(col == idx, -jnp.inf, work)
        vals.append(m)
        ti = jnp.where(lane == kk, idx.astype(jnp.int32), ti)
    es = [jnp.exp(m - vals[0]) for m in vals]
    tot = functools.reduce(lambda a, b: a + b, es)
    for kk in range(TOP_K):
        tv = jnp.where(lane == kk, es[kk] / tot, tv)
    ti_ref[...] = ti
    tg_ref[...] = tv


def post_mixer(prompt_mix, sample_mix, x, w_out, nf, rw, rb):
    tm = POST_TM
    assert tm == N_S
    d = D_MODEL
    last_p = N_P // tm - 1

    def rows(width):
        return pl.BlockSpec((tm, width), lambda i: (i, 0))

    def prows(width):
        return pl.BlockSpec((tm, width), lambda i: (jnp.minimum(i, last_p), 0))

    def full(shape):
        return pl.BlockSpec(shape, lambda i: (0,) * len(shape))

    widths = (HA_WIDTH, HB_WIDTH, HC_WIDTH)
    return pl.pallas_call(
        _post_body,
        grid=(N_ALL // tm,),
        in_specs=[prows(wd) for wd in widths] + [full((tm, wd)) for wd in widths]
                 + [rows(d), full((d, d)), full((1, d)), full((d, N_EXPERTS)), full((1, N_EXPERTS))],
        out_specs=[rows(d), rows(PACK_W), rows(ROUTE_LANES), rows(ROUTE_LANES)],
        out_shape=[jax.ShapeDtypeStruct((N_ALL, d), F32), jax.ShapeDtypeStruct((N_ALL, PACK_W), jnp.uint32),
                   jax.ShapeDtypeStruct((N_ALL, ROUTE_LANES), jnp.int32),
                   jax.ShapeDtypeStruct((N_ALL, ROUTE_LANES), F32)],
        compiler_params=_cparams(("parallel",)),
        name="post_mixer",
    )(*prompt_mix, *sample_mix, x, w_out, _row(nf), rw, _row(rb))


def _unpack_rows(words):
    lo = lax.bitcast_convert_type(words << 16, F32).astype(BF16)
    hi = lax.bitcast_convert_type(words & jnp.uint32(0xFFFF0000), F32).astype(BF16)
    return lo, hi


def _expert_tiles(ts_ref, nt_ref, src_ref, dst_ref, tn, in_buf, out_buf, sem_in, sem_out, compute):
    n, e = pl.program_id(0), pl.program_id(1)
    step = n * pl.num_programs(1) + e
    last_step = pl.num_programs(0) * pl.num_programs(1) - 1
    t0, nt = ts_ref[e], nt_ref[e]
    tm = MOE_TM
    nbuf = MOE_LOOKAHEAD + 1

    def in_copy(tile, slot):
        return pltpu.make_async_copy(src_ref.at[pl.ds(tile * tm, tm)], in_buf.at[slot], sem_in.at[slot])

    def out_copy(tile, slot):
        return pltpu.make_async_copy(out_buf.at[slot], dst_ref.at[pl.ds(tile * tm, tm), pl.ds(n * tn, tn)],
                                     sem_out.at[slot])

    def request_first_tiles(first_tile, count):
        for j in range(MOE_LOOKAHEAD):
            @pl.when(j < count)
            def _():
                in_copy(first_tile + j, j).start()

    @pl.when(step == 0)
    def _():
        request_first_tiles(t0, nt)

    def tile_body(i, carry):
        slot = i % nbuf
        in_copy(t0 + i, slot).wait()

        @pl.when(i + MOE_LOOKAHEAD < nt)
        def _():
            in_copy(t0 + i + MOE_LOOKAHEAD, (i + MOE_LOOKAHEAD) % nbuf).start()

        res = compute(in_buf[slot])
        oslot = i % 2

        @pl.when(i >= 2)
        def _():
            out_copy(t0 + i - 2, oslot).wait()

        out_buf[oslot] = res
        out_copy(t0 + i, oslot).start()
        return carry

    lax.fori_loop(0, nt, tile_body, 0)

    @pl.when(nt >= 2)
    def _():
        out_copy(t0 + nt - 2, nt % 2).wait()

    @pl.when(nt >= 1)
    def _():
        out_copy(t0 + nt - 1, (nt - 1) % 2).wait()

    e_next = jnp.where(e + 1 < pl.num_programs(1), e + 1, 0)

    @pl.when(step < last_step)
    def _():
        request_first_tiles(ts_ref[e_next], nt_ref[e_next])

    @pl.when(e == pl.num_programs(1) - 1)
    def _():
        out_buf[0] = jnp.zeros(out_buf.shape[1:], out_buf.dtype)

        def fill(tile, carry):
            out_copy(tile, 0).start()
            out_copy(tile, 0).wait()
            return carry

        lax.fori_loop(t0 + nt, MOE_TILES, fill, 0)


def _gmm1_body(ts_ref, nt_ref, xs_ref, wg_ref, wl_ref, bg_ref, bl_ref, act_ref,
               in_buf, out_buf, w_scr, sem_in, sem_out):
    tn = MOE_TN1

    @pl.when(nt_ref[pl.program_id(1)] > 0)
    def _():
        w_scr[:, :tn] = wg_ref[0, 0].astype(BF16)
        w_scr[:, tn:] = wl_ref[0, 0].astype(BF16)

    def compute(words):
        lo, hi = _unpack_rows(words)
        u = (jnp.dot(lo, w_scr[:PACK_W, :], preferred_element_type=F32)
             + jnp.dot(hi, w_scr[PACK_W:, :], preferred_element_type=F32))
        glu = jnp.minimum(u[:, :tn] + bg_ref[0, 0], SWIGLU_LIMIT)
        lin = jnp.clip(u[:, tn:] + bl_ref[0, 0], -SWIGLU_LIMIT, SWIGLU_LIMIT)
        return (glu * _sigmoid(SWIGLU_ALPHA * glu) * (lin + 1.0)).astype(BF16)

    _expert_tiles(ts_ref, nt_ref, xs_ref, act_ref, tn, in_buf, out_buf, sem_in, sem_out, compute)


def _gmm2_body(ts_ref, nt_ref, act_ref, w_ref, b_ref, yo_ref, in_buf, out_buf, w_scr, sem_in, sem_out):
    @pl.when(nt_ref[pl.program_id(1)] > 0)
    def _():
        w_scr[...] = w_ref[0, 0].astype(BF16)

    def compute(a):
        return jnp.dot(a, w_scr[...], preferred_element_type=F32) + b_ref[0, 0]

    _expert_tiles(ts_ref, nt_ref, act_ref, yo_ref, MOE_TN2, in_buf, out_buf, sem_in, sem_out, compute)


def _gmm_scratch(in_shape, in_dtype, tn, out_dtype):
    nbuf = MOE_LOOKAHEAD + 1
    return [pltpu.VMEM((nbuf,) + in_shape, in_dtype), pltpu.VMEM((2, MOE_TM, tn), out_dtype),
            pltpu.SemaphoreType.DMA((nbuf,)), pltpu.SemaphoreType.DMA((2,))]


def moe_gmm1(layer, xs, w1, b1, tile_start, tile_count):
    tn = MOE_TN1
    nb = D_FF // tn
    d = D_MODEL
    any_spec = pl.BlockSpec(memory_space=pl.ANY)
    in_buf, out_buf, sem_in, sem_out = _gmm_scratch((MOE_TM, PACK_W), jnp.uint32, tn, BF16)
    grid_spec = pltpu.PrefetchScalarGridSpec(
        num_scalar_prefetch=2,
        grid=(nb, N_EXPERTS),
        in_specs=[
            any_spec,
            pl.BlockSpec((1, 1, d, tn), lambda n, e, ts, nt: (layer, e, 0, n)),
            pl.BlockSpec((1, 1, d, tn), lambda n, e, ts, nt: (layer, e, 0, n + nb)),
            pl.BlockSpec((1, 1, 1, tn), lambda n, e, ts, nt: (layer, e, 0, n)),
            pl.BlockSpec((1, 1, 1, tn), lambda n, e, ts, nt: (layer, e, 0, n + nb)),
        ],
        out_specs=any_spec,
        scratch_shapes=[in_buf, out_buf, pltpu.VMEM((d, 2 * tn), BF16), sem_in, sem_out],
    )
    b1r = b1.reshape(DEPTH, N_EXPERTS, 1, 2 * D_FF)
    return pl.pallas_call(
        _gmm1_body,
        grid_spec=grid_spec,
        out_shape=jax.ShapeDtypeStruct((MOE_ROWS, D_FF), BF16),
        compiler_params=_cparams(("arbitrary", "arbitrary")),
        name="moe_gmm1",
    )(tile_start, tile_count, xs, w1, w1, b1r, b1r)


def moe_gmm2(layer, act, w2, b2, tile_start, tile_count):
    tn = MOE_TN2
    nb = D_MODEL // tn
    any_spec = pl.BlockSpec(memory_space=pl.ANY)
    in_buf, out_buf, sem_in, sem_out = _gmm_scratch((MOE_TM, D_FF), BF16, tn, F32)
    grid_spec = pltpu.PrefetchScalarGridSpec(
        num_scalar_prefetch=2,
        grid=(nb, N_EXPERTS),
        in_specs=[
            any_spec,
            pl.BlockSpec((1, 1, D_FF, tn), lambda n, e, ts, nt: (layer, e, 0, n)),
            pl.BlockSpec((1, 1, 1, tn), lambda n, e, ts, nt: (layer, e, 0, n)),
        ],
        out_specs=any_spec,
        scratch_shapes=[in_buf, out_buf, pltpu.VMEM((D_FF, tn), BF16), sem_in, sem_out],
    )
    return pl.pallas_call(
        _gmm2_body,
        grid_spec=grid_spec,
        out_shape=jax.ShapeDtypeStruct((MOE_ROWS, D_MODEL), F32),
        compiler_params=_cparams(("arbitrary", "arbitrary")),
        name="moe_gmm2",
    )(tile_start, tile_count, act, w2, b2.reshape(DEPTH, N_EXPERTS, 1, D_MODEL))


DISP_CHUNK = 256
N_SLOTS = N_ALL * TOP_K


def _dispatch_body(pos_ref, hp_ref, xs_in_ref, xs_ref, sem):
    del xs_in_ref
    base = pl.program_id(0) * (DISP_CHUNK * TOP_K)

    def row_copy(i, k):
        return pltpu.make_async_copy(hp_ref.at[pl.ds(i, 1)], xs_ref.at[pl.ds(pos_ref[base + i * TOP_K + k], 1)], sem)

    def all_rows(start):
        def body(i, carry):
            for k in range(TOP_K):
                cp = row_copy(i, k)
                cp.start() if start else cp.wait()
            return carry
        lax.fori_loop(0, DISP_CHUNK, body, 0, unroll=4)

    all_rows(True)
    all_rows(False)


def moe_dispatch(pos, hp):
    any_spec = pl.BlockSpec(memory_space=pl.ANY)
    grid_spec = pltpu.PrefetchScalarGridSpec(
        num_scalar_prefetch=1,
        grid=(N_ALL // DISP_CHUNK,),
        in_specs=[pl.BlockSpec((DISP_CHUNK, PACK_W), lambda i, pos: (i, 0)), any_spec],
        out_specs=any_spec,
        scratch_shapes=[pltpu.SemaphoreType.DMA(())],
    )
    xs0 = jnp.zeros((MOE_ROWS, PACK_W), jnp.uint32)
    return pl.pallas_call(
        _dispatch_body,
        grid_spec=grid_spec,
        out_shape=jax.ShapeDtypeStruct((MOE_ROWS, PACK_W), jnp.uint32),
        input_output_aliases={2: 0},
        compiler_params=_cparams(("arbitrary",)),
        name="moe_dispatch",
    )(pos, hp, xs0)


COMB_TB = 128


def _combine_body(final, pos_ref, yo_ref, x1_ref, tg_ref, nf_ref, o_ref, buf, sem):
    s = pl.program_id(0)
    ns = pl.num_programs(0)

    def row_copy(step, i, k, slot):
        j = (step * COMB_TB + i) * TOP_K + k
        return pltpu.make_async_copy(yo_ref.at[pl.ds(pos_ref[j], 1)], buf.at[slot, k, pl.ds(i, 1)], sem.at[slot])

    def start_tile(step, slot):
        def body(i, carry):
            for k in range(TOP_K):
                row_copy(step, i, k, slot).start()
            return carry
        lax.fori_loop(0, COMB_TB, body, 0, unroll=4)

    def wait_tile(slot):
        for k in range(TOP_K):
            pltpu.make_async_copy(yo_ref.at[pl.ds(0, COMB_TB)], buf.at[slot, k], sem.at[slot]).wait()

    @pl.when(s == 0)
    def _():
        start_tile(0, 0)

    @pl.when(s + 1 < ns)
    def _():
        start_tile(s + 1, (s + 1) % 2)

    slot = s % 2
    wait_tile(slot)
    acc = x1_ref[...]
    for k in range(TOP_K):
        acc = acc + tg_ref[:, k:k + 1] * buf[slot, k]
    o_ref[...] = _rms(acc, nf_ref[...]) if final else acc


def moe_combine(pos, yo, x1, tg, nf, final):
    tb = COMB_TB
    grid_spec = pltpu.PrefetchScalarGridSpec(
        num_scalar_prefetch=1,
        grid=(N_ALL // tb,),
        in_specs=[pl.BlockSpec(memory_space=pl.ANY),
                  pl.BlockSpec((tb, D_MODEL), lambda i, pos: (i, 0)),
                  pl.BlockSpec((tb, ROUTE_LANES), lambda i, pos: (i, 0)),
                  pl.BlockSpec((1, D_MODEL), lambda i, pos: (0, 0))],
        out_specs=pl.BlockSpec((tb, D_MODEL), lambda i, pos: (i, 0)),
        scratch_shapes=[pltpu.VMEM((2, TOP_K, tb, D_MODEL), F32), pltpu.SemaphoreType.DMA((2,))],
    )
    return pl.pallas_call(
        functools.partial(_combine_body, final),
        grid_spec=grid_spec,
        out_shape=jax.ShapeDtypeStruct((N_ALL, D_MODEL), F32),
        compiler_params=_cparams(("arbitrary",)),
        name="moe_combine",
    )(pos, yo, x1, tg, _row(nf))


def moe_layout(ti):
    flat_e = ti.reshape(-1)
    oh = (flat_e[:, None] == jnp.arange(N_EXPERTS, dtype=jnp.int32)[None, :]).astype(jnp.int32)
    csum = jnp.cumsum(oh, axis=0)
    counts = csum[-1]
    rank = jnp.sum((csum - oh) * oh, axis=1)
    tile_count = (counts + MOE_TM - 1) // MOE_TM
    tile_start = jnp.cumsum(tile_count) - tile_count
    pos = (tile_start * MOE_TM)[flat_e] + rank
    return pos.astype(jnp.int32), tile_start.astype(jnp.int32), tile_count.astype(jnp.int32)


def _block_diag(w):
    eye = jnp.eye(HB_BLOCKS, dtype=w.dtype)
    return jnp.einsum("hij,hg->higj", w, eye).reshape(HB_WIDTH, HB_WIDTH)


def kernel(x_prompt, x_sample, mem_prompt, state_hgrn, state_lru, state_conv, cache_mem_k, cache_mem_v,
           norm_mix, w_in, hgrn_lb, hgrn_onorm, conv_w, conv_b, lru_wa, lru_ba, lru_wx, lru_bx, lru_L,
           norm_lru, norm_xattn, norm_mem, w_mem_k, w_mem_v, w_out, norm_ffn, router_w, router_b,
           moe_w1, moe_b1, moe_w2, moe_b2, norm_final):
    x = jnp.concatenate([x_prompt.reshape(N_P, D_MODEL), x_sample.reshape(N_S, D_MODEL)], axis=0)
    mem = mem_prompt.reshape(BATCH * N_MEM, D_MODEL)
    cache_k = cache_mem_k.reshape(DEPTH, DEC_BATCH, N_MEM * HC_HEADS, HC_DH)
    cache_v = cache_mem_v.reshape(DEPTH, DEC_BATCH, N_MEM * HC_HEADS, HC_DH)
    conv_s = state_conv.reshape(DEPTH, DEC_BATCH, (CONV_W - 1) * HB_WIDTH)
    hgrn_s_out = jnp.zeros(state_hgrn.shape, F32)

    p_hgrn, p_lru, p_conv, p_mk, p_mv, s_lru, s_conv = [], [], [], [], [], [], []
    for l in range(DEPTH):
        w_mem = jnp.concatenate([w_mem_k[l], w_mem_v[l]], axis=1).astype(BF16)
        memkv = norm_matmul(mem, norm_mem[l], w_mem, tm=BATCH * N_MEM, tn=512)
        p_mk.append(memkv[:, :HC_WIDTH].reshape(BATCH, N_MEM, HC_HEADS, HC_DH))
        p_mv.append(memkv[:, HC_WIDTH:].reshape(BATCH, N_MEM, HC_HEADS, HC_DH))

        u = norm_matmul(x, norm_mix[l], w_in[l].astype(BF16), tm=N_ALL // 8, tn=512)

        oa_p, hg_p = hgrn_prompt(l, u, hgrn_lb, hgrn_onorm[l])
        oa_s, hgrn_s_out = hgrn_sample(l, u, hgrn_lb, hgrn_onorm[l], state_hgrn, hgrn_s_out)

        wa = _block_diag(lru_wa[l]).astype(BF16)
        wx = _block_diag(lru_wx[l]).astype(BF16)
        lru_args = (conv_w[l], conv_b[l], wa, wx, lru_ba[l], lru_bx[l], lru_L[l], norm_lru[l])
        ob_p, hl_p, cs_p = lru_prompt(u, *lru_args)
        ob_s, hl_s, cs_s = lru_sample(l, u, state_lru, conv_s, *lru_args)

        oc_p = xattn_prompt(u, memkv, norm_xattn[l])
        oc_s = xattn_sample(l, u, cache_k, cache_v, norm_xattn[l])

        x1, hp, ti, tg = post_mixer((oa_p, ob_p, oc_p), (oa_s, ob_s, oc_s), x, w_out[l].astype(BF16),
                                    norm_ffn[l], router_w[l], router_b[l])
        pos, tile_start, tile_count = moe_layout(ti[:, :TOP_K])
        xs = moe_dispatch(pos, hp)
        act = moe_gmm1(l, xs, moe_w1, moe_b1, tile_start, tile_count)
        yo = moe_gmm2(l, act, moe_w2, moe_b2, tile_start, tile_count)
        x = moe_combine(pos, yo, x1, tg, norm_final, final=(l == DEPTH - 1))

        p_hgrn.append(hg_p)
        p_lru.append(hl_p.reshape(BATCH, HB_WIDTH))
        p_conv.append(cs_p)
        s_lru.append(hl_s)
        s_conv.append(cs_s.reshape(DEC_BATCH, CONV_W - 1, HB_WIDTH))

    y_prompt = x[:N_P].reshape(BATCH, SEQ, D_MODEL)
    y_sample = x[N_P:].reshape(DEC_BATCH, DEC_SEQ, D_MODEL)
    return (y_prompt, y_sample, jnp.stack(p_hgrn), jnp.stack(p_lru), jnp.stack(p_conv),
            jnp.stack(p_mk), jnp.stack(p_mv), hgrn_s_out, jnp.stack(s_lru), jnp.stack(s_conv))
```

```python
import functools
import math

import jax
import jax.numpy as jnp
from jax import lax
from jax.experimental import pallas as pl
from jax.experimental.pallas import tpu as pltpu

F32 = jnp.float32
BF16 = jnp.bfloat16

D_MODEL = 2048
BATCH = 4
SEQ = 2048
DEPTH = 2
DEC_BATCH = 128
DEC_SEQ = 4
HA_DK = 128
HA_WIDTH = D_MODEL // 2
HA_HEADS = HA_WIDTH // HA_DK
HA_DV = HA_WIDTH // HA_HEADS
HB_WIDTH = D_MODEL // 4
HB_BLOCKS = 8
HB_BW = HB_WIDTH // HB_BLOCKS
CONV_W = 4
LRU_C = 8.0
HC_HEADS = 4
HC_WIDTH = D_MODEL - HA_WIDTH - HB_WIDTH
HC_DH = HC_WIDTH // HC_HEADS
N_MEM = 256
IN_COLS = 4 * HA_WIDTH + 2 * HB_WIDTH + HC_WIDTH
N_EXPERTS = 32
TOP_K = 4
D_FF = D_MODEL
SWIGLU_LIMIT = 7.0
SWIGLU_ALPHA = 1.702
EPS = 1e-6

N_P = BATCH * SEQ
N_S = DEC_BATCH * DEC_SEQ
N_ALL = N_P + N_S

COL_QA, COL_FA, COL_IA, COL_GA = 0, 1, 2, 3
COL_XB, COL_GB, COL_QC = 8, 9, 10

HGRN_CHUNK = 64
HGRN_SUB = 16
LRU_TB = 256
XATTN_TQ = 512
MOE_TM = 256
MOE_LOOKAHEAD = 3
TILE_DMA_PRIORITY = 1
MOE_TN1 = 512
MOE_TN2 = 1024
MOE_TILES = (N_ALL * TOP_K) // MOE_TM + N_EXPERTS
MOE_ROWS = MOE_TILES * MOE_TM
VMEM_LIMIT = 56 * 1024 * 1024

NT_DIMS = (((1,), (1,)), ((), ()))
TN_DIMS = (((0,), (0,)), ((), ()))


def _cparams(sem):
    return pltpu.CompilerParams(dimension_semantics=sem, vmem_limit_bytes=VMEM_LIMIT)


def _rms(x, g):
    return x * lax.rsqrt(jnp.mean(x * x, axis=-1, keepdims=True) + EPS) * g


def _sigmoid(x):
    return jax.nn.sigmoid(x)


def _silu(x):
    return x * jax.nn.sigmoid(x)


def _norm_matmul_body(x_ref, g_ref, w_ref, o_ref, xn_ref):
    @pl.when(pl.program_id(1) == 0)
    def _():
        xn_ref[...] = _rms(x_ref[...], g_ref[...]).astype(BF16)

    o_ref[...] = jnp.dot(xn_ref[...], w_ref[...], preferred_element_type=F32)


def norm_matmul(x, g, w, tm, tn):
    n, d = x.shape
    nc = w.shape[1]
    return pl.pallas_call(
        _norm_matmul_body,
        grid=(n // tm, nc // tn),
        in_specs=[
            pl.BlockSpec((tm, d), lambda i, j: (i, 0)),
            pl.BlockSpec((1, d), lambda i, j: (0, 0)),
            pl.BlockSpec((d, tn), lambda i, j: (0, j)),
        ],
        out_specs=pl.BlockSpec((tm, tn), lambda i, j: (i, j)),
        out_shape=jax.ShapeDtypeStruct((n, nc), F32),
        scratch_shapes=[pltpu.VMEM((tm, d), BF16)],
        compiler_params=_cparams(("parallel", "arbitrary")),
        name="norm_matmul",
    )(x, g.reshape(1, d), w)


def _hgrn_lower_bound(lbp, layer):
    rows = [lbp[r:r + 1, :] for r in range(DEPTH)]
    m = functools.reduce(jnp.maximum, rows)
    es = [jnp.exp(r - m) for r in rows]
    tot = functools.reduce(lambda a, b: a + b, es)
    lb = jnp.zeros_like(m)
    for r in range(1, layer + 1):
        lb = lb + es[r] / tot
    return lb


def _hgrn_gates(z, qa, lb):
    f = lb + (1.0 - lb) * _sigmoid(z)
    k = (1.0 - lb) * _sigmoid(-z)
    q = _silu(qa)
    return f, k, q


def _hgrn_prompt_body(layer, qa_ref, fa_ref, ia_ref, ga_ref, lbp_ref, on_ref, oa_ref, st_ref, s_scr):
    c = pl.program_id(1)
    C, SUB = HGRN_CHUNK, HGRN_SUB
    nsub = C // SUB

    @pl.when(c == 0)
    def _():
        s_scr[...] = jnp.zeros_like(s_scr)

    lb = _hgrn_lower_bound(lbp_ref[...], layer)
    f, k, q = _hgrn_gates(fa_ref[...], qa_ref[...], lb)
    g = jnp.log(f)
    v = ia_ref[...]
    gate = _silu(ga_ref[...])

    row = lax.broadcasted_iota(jnp.int32, (C, C), 0)
    col = lax.broadcasted_iota(jnp.int32, (C, C), 1)
    tri = (row >= col).astype(F32)
    b_all = jnp.dot(tri, g, preferred_element_type=F32, precision=lax.Precision.HIGHEST)

    row_sub = lax.broadcasted_iota(jnp.int32, (SUB, HA_DK), 0)
    row_c = lax.broadcasted_iota(jnp.int32, (C, HA_DK), 0)
    lane_c = lax.broadcasted_iota(jnp.int32, (SUB, C), 1)
    neg_inf = jnp.float32(-jnp.inf)

    states = [s_scr[h] for h in range(HA_HEADS)]
    new_states, outs = [], []
    for h in range(HA_HEADS):
        sl = slice(h * HA_DK, (h + 1) * HA_DK)
        bh, qh, kh, vh = b_all[:, sl], q[:, sl], k[:, sl], v[:, sl]
        kh16 = kh.astype(BF16)
        vh16 = vh.astype(BF16)
        b_end = bh[C - 1:C, :]
        st = states[h]
        o = lax.dot_general((qh * jnp.exp(bh)).astype(BF16), st.astype(BF16), NT_DIMS,
                            preferred_element_type=F32)
        kdec = (kh * jnp.exp(b_end - bh)).astype(BF16)
        upd = lax.dot_general(vh16, kdec, TN_DIMS, preferred_element_type=F32)
        new_states.append(st * jnp.exp(b_end) + upd)

        a_rows = []
        for i in range(nsub):
            bi = bh[i * SUB:(i + 1) * SUB, :]
            qi = qh[i * SUB:(i + 1) * SUB, :]
            parts = []
            for s in range(SUB):
                dec = jnp.exp(jnp.where(row_sub >= s, bi - bi[s:s + 1, :], neg_inf))
                parts.append(qi * dec)
            q_all = jnp.concatenate(parts, axis=0).astype(BF16)
            m = lax.dot_general(q_all, kh16, NT_DIMS, preferred_element_type=F32)
            a_i = jnp.zeros((SUB, C), F32)
            for s in range(SUB):
                a_i = a_i + jnp.where(lane_c == i * SUB + s, m[s * SUB:(s + 1) * SUB, :], 0.0)
            if i > 0:
                b_prev = bh[i * SUB - 1:i * SUB, :]
                q_i = (qi * jnp.exp(bi - b_prev)).astype(BF16)
                k_i = (kh * jnp.exp(jnp.where(row_c < i * SUB, b_prev - bh, neg_inf))).astype(BF16)
                a_i = a_i + lax.dot_general(q_i, k_i, NT_DIMS, preferred_element_type=F32)
            a_rows.append(a_i)
        att = jnp.concatenate(a_rows, axis=0).astype(BF16)
        o = o + jnp.dot(att, vh16, preferred_element_type=F32)
        outs.append(_rms(o, on_ref[...]) * gate[:, sl])

    for h in range(HA_HEADS):
        oa_ref[:, h * HA_DK:(h + 1) * HA_DK] = outs[h]
        s_scr[h] = new_states[h]

    @pl.when(c == pl.num_programs(1) - 1)
    def _():
        for h in range(HA_HEADS):
            st_ref[0, h] = new_states[h].T


def hgrn_prompt(layer, u, lbp, onorm):
    nchunk = SEQ // HGRN_CHUNK

    def spec(colblk):
        return pl.BlockSpec((HGRN_CHUNK, HA_WIDTH), lambda b, c: (b * nchunk + c, colblk))

    return pl.pallas_call(
        functools.partial(_hgrn_prompt_body, layer),
        grid=(BATCH, nchunk),
        in_specs=[spec(COL_QA), spec(COL_FA), spec(COL_IA), spec(COL_GA),
                  pl.BlockSpec((DEPTH, HA_WIDTH), lambda b, c: (0, 0)),
                  pl.BlockSpec((1, HA_DV), lambda b, c: (0, 0))],
        out_specs=[pl.BlockSpec((HGRN_CHUNK, HA_WIDTH), lambda b, c: (b * nchunk + c, 0)),
                   pl.BlockSpec((1, HA_HEADS, HA_DK, HA_DV), lambda b, c: (b, 0, 0, 0))],
        out_shape=[jax.ShapeDtypeStruct((N_P, HA_WIDTH), F32),
                   jax.ShapeDtypeStruct((BATCH, HA_HEADS, HA_DK, HA_DV), F32)],
        scratch_shapes=[pltpu.VMEM((HA_HEADS, HA_DV, HA_DK), F32)],
        compiler_params=_cparams(("parallel", "arbitrary")),
        name="hgrn_prompt",
    )(u, u, u, u, lbp, onorm.reshape(1, HA_DV))


HGRN_SB = 2


def _hgrn_sample_body(layer, qa_ref, fa_ref, ia_ref, ga_ref, lbp_ref, on_ref, s_ref, so_in_ref, oa_ref, so_ref):
    del so_in_ref
    assert HGRN_SB == 2 and DEC_SEQ >= 3
    R = HGRN_SB * DEC_SEQ
    lb = _hgrn_lower_bound(lbp_ref[...], layer)
    f, k, q = _hgrn_gates(fa_ref[...], qa_ref[...], lb)
    g = jnp.log(f)
    v = ia_ref[...]
    gate = _silu(ga_ref[...])

    ri = lax.broadcasted_iota(jnp.int32, (R, R), 0)
    ci = lax.broadcasted_iota(jnp.int32, (R, R), 1)
    same_elem = (ri >= DEC_SEQ) == (ci >= DEC_SEQ)
    tri = (same_elem & (ri >= ci)).astype(F32)
    b = jnp.dot(tri, g, preferred_element_type=F32, precision=lax.Precision.HIGHEST)
    row_w = lax.broadcasted_iota(jnp.int32, b.shape, 0)
    b_end = jnp.where(row_w < DEC_SEQ, b[DEC_SEQ - 1:DEC_SEQ, :], b[R - 1:R, :])
    qe = q * jnp.exp(b)
    kdec = k * jnp.exp(b_end - b)
    e_end = jnp.exp(b_end)

    row = lax.broadcasted_iota(jnp.int32, (R, HA_DK), 0)
    elem = (row >= DEC_SEQ).astype(jnp.int32)
    trow = row - elem * DEC_SEQ
    lane_r = lax.broadcasted_iota(jnp.int32, (R, R), 1)
    neg_inf = jnp.float32(-jnp.inf)

    states = [[s_ref[0, bb, h] for h in range(HA_HEADS)] for bb in range(HGRN_SB)]
    new_states = [[None] * HA_HEADS for _ in range(HGRN_SB)]
    outs = []
    for h in range(HA_HEADS):
        sl = slice(h * HA_DK, (h + 1) * HA_DK)
        bh, qh, kh, vh = b[:, sl], q[:, sl], k[:, sl], v[:, sl]
        vh16 = vh.astype(BF16)
        parts = []
        for s in range(R):
            valid = (elem == s // DEC_SEQ) & (row >= s)
            parts.append(qh * jnp.exp(jnp.where(valid, bh - bh[s:s + 1, :], neg_inf)))
        q_all = jnp.concatenate(parts, axis=0).astype(BF16)
        m = lax.dot_general(q_all, kh.astype(BF16), NT_DIMS, preferred_element_type=F32)
        att = jnp.zeros((R, R), F32)
        for s in range(R):
            att = att + jnp.where(lane_r == s, m[s * R:(s + 1) * R, :], 0.0)
        o = jnp.dot(att.astype(BF16), vh16, preferred_element_type=F32)

        qe16 = qe[:, sl].astype(BF16)
        for bb in range(HGRN_SB):
            s0 = states[bb][h]
            o = o + jnp.where(elem == bb, jnp.dot(qe16, s0.astype(BF16), preferred_element_type=F32), 0.0)
            e = e_end[bb * DEC_SEQ:bb * DEC_SEQ + 1, sl]
            e1 = e.astype(BF16).astype(F32)
            e2 = (e - e1).astype(BF16).astype(F32)
            e3 = e - e1 - e2
            eparts = jnp.where(trow == 0, e1, jnp.where(trow == 1, e2, jnp.where(trow == 2, e3, 0.0)))
            own = elem == bb
            lhs = jnp.where(own, kdec[:, sl], eparts).astype(BF16)
            ones = jnp.where(own | (trow > 2), 0.0, 1.0)
            rhs = jnp.concatenate([ones, jnp.where(own, vh, 0.0)], axis=-1).astype(BF16)
            res = lax.dot_general(lhs, rhs, TN_DIMS, preferred_element_type=F32)
            new_states[bb][h] = res[:, :HA_DV] * s0 + res[:, HA_DV:]
        outs.append(_rms(o, on_ref[...]) * gate[:, sl])

    for h in range(HA_HEADS):
        oa_ref[:, h * HA_DK:(h + 1) * HA_DK] = outs[h]
        for bb in range(HGRN_SB):
            so_ref[0, bb, h] = new_states[bb][h]


def hgrn_sample(layer, u, lbp, onorm, state, so_buf):
    rows8 = HGRN_SB * DEC_SEQ
    row0 = N_P // rows8

    def spec(colblk):
        return pl.BlockSpec((rows8, HA_WIDTH), lambda i: (row0 + i, colblk))

    st_spec = pl.BlockSpec((1, HGRN_SB, HA_HEADS, HA_DK, HA_DV), lambda i: (layer, i, 0, 0, 0))
    return pl.pallas_call(
        functools.partial(_hgrn_sample_body, layer),
        grid=(DEC_BATCH // HGRN_SB,),
        in_specs=[spec(COL_QA), spec(COL_FA), spec(COL_IA), spec(COL_GA),
                  pl.BlockSpec((DEPTH, HA_WIDTH), lambda i: (0, 0)),
                  pl.BlockSpec((1, HA_DV), lambda i: (0, 0)),
                  st_spec, pl.BlockSpec(memory_space=pl.ANY)],
        out_specs=[pl.BlockSpec((rows8, HA_WIDTH), lambda i: (i, 0)), st_spec],
        out_shape=[jax.ShapeDtypeStruct((N_S, HA_WIDTH), F32), jax.ShapeDtypeStruct(state.shape, F32)],
        input_output_aliases={7: 1},
        compiler_params=_cparams(("parallel",)),
        name="hgrn_sample",
    )(u, u, u, u, lbp, onorm.reshape(1, HA_DV), state, so_buf)


def _softplus(x):
    return jnp.maximum(x, 0.0) + jnp.log1p(jnp.exp(-jnp.abs(x)))


def _lru_gates(xc, wa_ref, wx_ref, ba, bx, sp):
    xc16 = xc.astype(BF16)
    r = _sigmoid(jnp.dot(xc16, wa_ref[...], preferred_element_type=F32) + ba)
    i = _sigmoid(jnp.dot(xc16, wx_ref[...], preferred_element_type=F32) + bx)
    log_a = -LRU_C * r * sp
    a = jnp.exp(log_a)
    th = jnp.tanh(log_a)
    mult = jnp.sqrt(2.0 * th / (th - 1.0))
    return a, mult * i * xc


def _lru_prompt_body(xb_ref, gb_ref, cw_ref, cb_ref, wa_ref, wx_ref, ba_ref, bx_ref, l_ref, nl_ref,
                     ob_ref, hl_ref, cs_ref, xe_scr, h_scr):
    t = pl.program_id(1)
    tb = LRU_TB

    @pl.when(t == 0)
    def _():
        xe_scr[0:8, :] = jnp.zeros((8, HB_WIDTH), F32)
        h_scr[...] = jnp.zeros_like(h_scr)

    xb = xb_ref[...]
    xe_scr[8:8 + tb, :] = xb
    cw = cw_ref[...]
    xc = cb_ref[...] + cw[3:4, :] * xb
    for j in range(1, CONV_W):
        xc = xc + cw[3 - j:4 - j, :] * xe_scr[pl.ds(8 - j, tb), :]
    xe_scr[0:8, :] = xb[tb - 8:tb, :]

    sp = _softplus(-l_ref[...])
    a, bt = _lru_gates(xc, wa_ref, wx_ref, ba_ref[...], bx_ref[...], sp)
    row = lax.broadcasted_iota(jnp.int32, (tb, HB_WIDTH), 0)
    sh = 1
    while sh < tb:
        keep = row >= sh
        a_sh = jnp.where(keep, pltpu.roll(a, sh, 0), 1.0)
        b_sh = jnp.where(keep, pltpu.roll(bt, sh, 0), 0.0)
        bt = a * b_sh + bt
        a = a * a_sh
        sh *= 2
    hcur = bt + a * h_scr[0:1, :]
    h_last = hcur[tb - 1:tb, :]
    h_scr[...] = jnp.broadcast_to(h_last, h_scr.shape)
    y = hcur * jax.nn.gelu(gb_ref[...])
    ob_ref[...] = _rms(y, nl_ref[...])
    hl_ref[0] = h_last
    cs_ref[0] = xb[tb - (CONV_W - 1):tb, :]


def _row(p):
    return p.reshape(1, -1)


def lru_prompt(u, cw, cb, wa, wx, ba, bx, lam, nl):
    nt = SEQ // LRU_TB
    w = HB_WIDTH

    def uspec(colblk):
        return pl.BlockSpec((LRU_TB, w), lambda b, t: (b * nt + t, colblk))

    def full(shape):
        return pl.BlockSpec(shape, lambda b, t: (0,) * len(shape))

    return pl.pallas_call(
        _lru_prompt_body,
        grid=(BATCH, nt),
        in_specs=[uspec(COL_XB), uspec(COL_GB), full((CONV_W, w)), full((1, w)), full((w, w)), full((w, w)),
                  full((1, w)), full((1, w)), full((1, w)), full((1, w))],
        out_specs=[pl.BlockSpec((LRU_TB, w), lambda b, t: (b * nt + t, 0)),
                   pl.BlockSpec((1, 1, w), lambda b, t: (b, 0, 0)),
                   pl.BlockSpec((1, CONV_W - 1, w), lambda b, t: (b, 0, 0))],
        out_shape=[jax.ShapeDtypeStruct((N_P, w), F32),
                   jax.ShapeDtypeStruct((BATCH, 1, w), F32),
                   jax.ShapeDtypeStruct((BATCH, CONV_W - 1, w), F32)],
        scratch_shapes=[pltpu.VMEM((LRU_TB + 8, w), F32), pltpu.VMEM((8, w), F32)],
        compiler_params=_cparams(("parallel", "arbitrary")),
        name="lru_prompt",
    )(u, u, cw, _row(cb), wa, wx, _row(ba), _row(bx), _row(lam), _row(nl))


def _lru_sample_body(xb_ref, gb_ref, h0_ref, cv_ref, cw_ref, cb_ref, wa_ref, wx_ref, ba_ref, bx_ref, l_ref, nl_ref,
                     ob_ref, hl_ref, cs_ref, x_scr, g_scr, y_scr):
    w = HB_WIDTH
    nb = DEC_BATCH
    nchunk = w // 128
    cw = cw_ref[...]
    for c in range(nchunk):
        x_scr[c] = xb_ref[:, c * 128:(c + 1) * 128]
        g_scr[c] = gb_ref[:, c * 128:(c + 1) * 128]

    def time_rows(scr, t):
        return jnp.concatenate([scr[c, pl.ds(t, nb, stride=DEC_SEQ), :] for c in range(nchunk)], axis=-1)

    xs = [cv_ref[0, :, j * w:(j + 1) * w] for j in range(CONV_W - 1)]
    xs += [time_rows(x_scr, t) for t in range(DEC_SEQ)]
    sp = _softplus(-l_ref[...])
    hcur = h0_ref[0]
    for t in range(DEC_SEQ):
        xc = cb_ref[...]
        for j in range(CONV_W):
            xc = xc + cw[j:j + 1, :] * xs[t + j]
        a, bt = _lru_gates(xc, wa_ref, wx_ref, ba_ref[...], bx_ref[...], sp)
        hcur = a * hcur + bt
        y = _rms(hcur * jax.nn.gelu(time_rows(g_scr, t)), nl_ref[...])
        for c in range(nchunk):
            y_scr[c, pl.ds(t, nb, stride=DEC_SEQ), :] = y[:, c * 128:(c + 1) * 128]
    for c in range(nchunk):
        ob_ref[:, c * 128:(c + 1) * 128] = y_scr[c]
    hl_ref[...] = hcur
    for j in range(CONV_W - 1):
        cs_ref[:, j * w:(j + 1) * w] = xs[DEC_SEQ + j]


def lru_sample(layer, u, h0, conv, cw, cb, wa, wx, ba, bx, lam, nl):
    w = HB_WIDTH
    row0 = N_P // N_S

    def full(shape):
        return pl.BlockSpec(shape, lambda i: (0,) * len(shape))

    return pl.pallas_call(
        _lru_sample_body,
        grid=(1,),
        in_specs=[pl.BlockSpec((N_S, w), lambda i: (row0, COL_XB)),
                  pl.BlockSpec((N_S, w), lambda i: (row0, COL_GB)),
                  pl.BlockSpec((1, DEC_BATCH, w), lambda i: (layer, 0, 0)),
                  pl.BlockSpec((1, DEC_BATCH, (CONV_W - 1) * w), lambda i: (layer, 0, 0)),
                  full((CONV_W, w)), full((1, w)), full((w, w)), full((w, w)),
                  full((1, w)), full((1, w)), full((1, w)), full((1, w))],
        out_specs=[full((N_S, w)), full((DEC_BATCH, w)), full((DEC_BATCH, (CONV_W - 1) * w))],
        out_shape=[jax.ShapeDtypeStruct((N_S, w), F32),
                   jax.ShapeDtypeStruct((DEC_BATCH, w), F32),
                   jax.ShapeDtypeStruct((DEC_BATCH, (CONV_W - 1) * w), F32)],
        scratch_shapes=[pltpu.VMEM((w // 128, N_S, 128), F32)] * 3,
        compiler_params=_cparams(("arbitrary",)),
        name="lru_sample",
    )(u, u, h0, conv, cw, _row(cb), wa, wx, _row(ba), _row(bx), _row(lam), _row(nl))


XATTN_SCALE = 1.0 / math.sqrt(HC_DH)


def _xattn_prompt_body(q_ref, k_ref, v_ref, nx_ref, oc_ref):
    q = q_ref[...]
    outs = []
    for h in range(HC_HEADS):
        sl = slice(h * HC_DH, (h + 1) * HC_DH)
        s = lax.dot_general(q[:, sl].astype(BF16), k_ref[0, :, sl].astype(BF16), NT_DIMS,
                            preferred_element_type=F32) * XATTN_SCALE
        p = jnp.exp(s - jnp.max(s, axis=-1, keepdims=True))
        p = p / jnp.sum(p, axis=-1, keepdims=True)
        outs.append(jnp.dot(p.astype(BF16), v_ref[0, :, sl].astype(BF16), preferred_element_type=F32))
    oc_ref[...] = _rms(jnp.concatenate(outs, axis=-1), nx_ref[...])


def xattn_prompt(u, memkv, nx):
    nq = SEQ // XATTN_TQ
    w = HC_WIDTH
    kv = memkv.reshape(BATCH, N_MEM, 2 * w)
    return pl.pallas_call(
        _xattn_prompt_body,
        grid=(BATCH, nq),
        in_specs=[pl.BlockSpec((XATTN_TQ, w), lambda b, t: (b * nq + t, COL_QC)),
                  pl.BlockSpec((1, N_MEM, w), lambda b, t: (b, 0, 0)),
                  pl.BlockSpec((1, N_MEM, w), lambda b, t: (b, 0, 1)),
                  pl.BlockSpec((1, w), lambda b, t: (0, 0))],
        out_specs=pl.BlockSpec((XATTN_TQ, w), lambda b, t: (b * nq + t, 0)),
        out_shape=jax.ShapeDtypeStruct((N_P, w), F32),
        compiler_params=_cparams(("parallel", "parallel")),
        name="xattn_prompt",
    )(u, kv, kv, _row(nx))


XATTN_SB = 8


def _xattn_sample_body(q_ref, k_ref, v_ref, nx_ref, oc_ref):
    assert HC_HEADS == 4 and DEC_SEQ == 4
    q = q_ref[...]
    nrow = N_MEM * HC_HEADS
    ncol = HC_HEADS * 8
    row8 = lax.broadcasted_iota(jnp.int32, (8, HC_DH), 0)
    mem_head = lax.broadcasted_iota(jnp.int32, (nrow, ncol), 0) & (HC_HEADS - 1)
    col_head = lax.shift_right_logical(lax.broadcasted_iota(jnp.int32, (nrow, ncol), 1), 3)
    own_head = mem_head == col_head
    pair_rows = []
    for pair in range(XATTN_SB // 2):
        q_all = jnp.concatenate([q[pair * 8:(pair + 1) * 8, h * HC_DH:(h + 1) * HC_DH] for h in range(HC_HEADS)],
                                axis=0).astype(BF16)
        o_all = []
        for e in range(2):
            bb = pair * 2 + e
            st = lax.dot_general(k_ref[0, bb].astype(BF16), q_all, NT_DIMS, preferred_element_type=F32)
            st = jnp.where(own_head, st * XATTN_SCALE, -jnp.inf)
            pt = jnp.exp(st - jnp.max(st, axis=0, keepdims=True))
            pt = pt / jnp.sum(pt, axis=0, keepdims=True)
            o_all.append(lax.dot_general(pt.astype(BF16), v_ref[0, bb].astype(BF16), TN_DIMS,
                                         preferred_element_type=F32))
        head_cols = [jnp.where(row8 < DEC_SEQ, o_all[0][h * 8:(h + 1) * 8], o_all[1][h * 8:(h + 1) * 8])
                     for h in range(HC_HEADS)]
        pair_rows.append(jnp.concatenate(head_cols, axis=-1))
    oc_ref[...] = _rms(jnp.concatenate(pair_rows, axis=0), nx_ref[...])


def xattn_sample(layer, u, cache_k, cache_v, nx):
    w = HC_WIDTH
    rows = XATTN_SB * DEC_SEQ
    row0 = N_P // rows
    kv_spec = pl.BlockSpec((1, XATTN_SB, N_MEM * HC_HEADS, HC_DH), lambda i: (layer, i, 0, 0))
    return pl.pallas_call(
        _xattn_sample_body,
        grid=(DEC_BATCH // XATTN_SB,),
        in_specs=[pl.BlockSpec((rows, w), lambda i: (row0 + i, COL_QC)), kv_spec, kv_spec,
                  pl.BlockSpec((1, w), lambda i: (0, 0))],
        out_specs=pl.BlockSpec((rows, w), lambda i: (i, 0)),
        out_shape=jax.ShapeDtypeStruct((N_S, w), F32),
        compiler_params=_cparams(("parallel",)),
        name="xattn_sample",
    )(u, cache_k, cache_v, _row(nx))


POST_TM = 512
ROUTE_LANES = 128
PACK_W = D_MODEL // 2


def _post_body(oap_ref, obp_ref, ocp_ref, oas_ref, obs_ref, ocs_ref, x_ref, w_ref, nf_ref, rw_ref, rb_ref,
               x1_ref, hn_ref, ti_ref, tg_ref):
    is_sample = pl.program_id(0) == pl.num_programs(0) - 1

    def mixed(p_ref, s_ref):
        return jnp.where(is_sample, s_ref[...], p_ref[...]).astype(BF16)

    mix = jnp.concatenate([mixed(oap_ref, oas_ref), mixed(obp_ref, obs_ref), mixed(ocp_ref, ocs_ref)], axis=-1)
    x1 = x_ref[...] + jnp.dot(mix, w_ref[...], preferred_element_type=F32)
    x1_ref[...] = x1
    hn = _rms(x1, nf_ref[...])
    bits = lax.bitcast_convert_type(hn.astype(BF16).astype(F32), jnp.uint32)
    hn_ref[...] = (bits[:, :PACK_W] >> 16) | (bits[:, PACK_W:] & jnp.uint32(0xFFFF0000))
    hn_hi = hn.astype(BF16)
    hn_lo = (hn - hn_hi.astype(F32)).astype(BF16)
    rw = rw_ref[...]
    rw_hi = rw.astype(BF16)
    rw_lo = (rw - rw_hi.astype(F32)).astype(BF16)
    logits = (jnp.dot(hn_hi, rw_hi, preferred_element_type=F32) + jnp.dot(hn_hi, rw_lo, preferred_element_type=F32)
              + jnp.dot(hn_lo, rw_hi, preferred_element_type=F32) + rb_ref[...])
    tm = logits.shape[0]
    col = lax.broadcasted_iota(jnp.int32, (tm, N_EXPERTS), 1).astype(F32)
    lane = lax.broadcasted_iota(jnp.int32, (tm, ROUTE_LANES), 1)
    work = logits
    ti = jnp.zeros((tm, ROUTE_LANES), jnp.int32)
    tv = jnp.zeros((tm, ROUTE_LANES), F32)
    vals = []
    for kk in range(TOP_K):
        m = jnp.max(work, axis=-1, keepdims=True)
        idx = jnp.min(jnp.where(work == m, col, float(N_EXPERTS)), axis=-1, keepdims=True)
        work = jnp.where(col == idx, -jnp.inf, work)
        vals.append(m)
        ti = jnp.where(lane == kk, idx.astype(jnp.int32), ti)
    es = [jnp.exp(m - vals[0]) for m in vals]
    tot = functools.reduce(lambda a, b: a + b, es)
    for kk in range(TOP_K):
        tv = jnp.where(lane == kk, es[kk] / tot, tv)
    ti_ref[...] = ti
    tg_ref[...] = tv


def post_mixer(prompt_mix, sample_mix, x, w_out, nf, rw, rb):
    tm = POST_TM
    assert tm == N_S
    d = D_MODEL
    last_p = N_P // tm - 1

    def rows(width):
        return pl.BlockSpec((tm, width), lambda i: (i, 0))

    def prows(width):
        return pl.BlockSpec((tm, width), lambda i: (jnp.minimum(i, last_p), 0))

    def full(shape):
        return pl.BlockSpec(shape, lambda i: (0,) * len(shape))

    widths = (HA_WIDTH, HB_WIDTH, HC_WIDTH)
    return pl.pallas_call(
        _post_body,
        grid=(N_ALL // tm,),
        in_specs=[prows(wd) for wd in widths] + [full((tm, wd)) for wd in widths]
                 + [rows(d), full((d, d)), full((1, d)), full((d, N_EXPERTS)), full((1, N_EXPERTS))],
        out_specs=[rows(d), rows(PACK_W), rows(ROUTE_LANES), rows(ROUTE_LANES)],
        out_shape=[jax.ShapeDtypeStruct((N_ALL, d), F32), jax.ShapeDtypeStruct((N_ALL, PACK_W), jnp.uint32),
                   jax.ShapeDtypeStruct((N_ALL, ROUTE_LANES), jnp.int32),
                   jax.ShapeDtypeStruct((N_ALL, ROUTE_LANES), F32)],
        compiler_params=_cparams(("parallel",)),
        name="post_mixer",
    )(*prompt_mix, *sample_mix, x, w_out, _row(nf), rw, _row(rb))


def _unpack_rows(words):
    lo = lax.bitcast_convert_type(words << 16, F32).astype(BF16)
    hi = lax.bitcast_convert_type(words & jnp.uint32(0xFFFF0000), F32).astype(BF16)
    return lo, hi


def _expert_tiles(ts_ref, nt_ref, src_ref, dst_ref, tn, in_buf, out_buf, sem_in, sem_out, compute):
    n, e = pl.program_id(0), pl.program_id(1)
    step = n * pl.num_programs(1) + e
    last_step = pl.num_programs(0) * pl.num_programs(1) - 1
    t0, nt = ts_ref[e], nt_ref[e]
    tm = MOE_TM
    nbuf = MOE_LOOKAHEAD + 1

    def in_copy(tile, slot):
        return pltpu.make_async_copy(src_ref.at[pl.ds(tile * tm, tm)], in_buf.at[slot], sem_in.at[slot])

    def out_copy(tile, slot):
        return pltpu.make_async_copy(out_buf.at[slot], dst_ref.at[pl.ds(tile * tm, tm), pl.ds(n * tn, tn)],
                                     sem_out.at[slot])

    def request_first_tiles(first_tile, count):
        for j in range(MOE_LOOKAHEAD):
            @pl.when(j < count)
            def _():
                in_copy(first_tile + j, j).start(priority=TILE_DMA_PRIORITY)

    @pl.when(step == 0)
    def _():
        request_first_tiles(t0, nt)

    def tile_body(i, carry):
        slot = i % nbuf
        in_copy(t0 + i, slot).wait()

        @pl.when(i + MOE_LOOKAHEAD < nt)
        def _():
            in_copy(t0 + i + MOE_LOOKAHEAD, (i + MOE_LOOKAHEAD) % nbuf).start(priority=TILE_DMA_PRIORITY)

        res = compute(in_buf[slot])
        oslot = i % 2

        @pl.when(i >= 2)
        def _():
            out_copy(t0 + i - 2, oslot).wait()

        out_buf[oslot] = res
        out_copy(t0 + i, oslot).start(priority=TILE_DMA_PRIORITY)
        return carry

    lax.fori_loop(0, nt, tile_body, 0)

    @pl.when(nt >= 2)
    def _():
        out_copy(t0 + nt - 2, nt % 2).wait()

    @pl.when(nt >= 1)
    def _():
        out_copy(t0 + nt - 1, (nt - 1) % 2).wait()

    e_next = jnp.where(e + 1 < pl.num_programs(1), e + 1, 0)

    @pl.when(step < last_step)
    def _():
        request_first_tiles(ts_ref[e_next], nt_ref[e_next])

    @pl.when(e == pl.num_programs(1) - 1)
    def _():
        out_buf[0] = jnp.zeros(out_buf.shape[1:], out_buf.dtype)

        def fill(tile, carry):
            out_copy(tile, 0).start()
            out_copy(tile, 0).wait()
            return carry

        lax.fori_loop(t0 + nt, MOE_TILES, fill, 0)


def _gmm1_body(ts_ref, nt_ref, xs_ref, wg_ref, wl_ref, bg_ref, bl_ref, act_ref,
               in_buf, out_buf, w_scr, sem_in, sem_out):
    tn = MOE_TN1

    @pl.when(nt_ref[pl.program_id(1)] > 0)
    def _():
        w_scr[:, :tn] = wg_ref[0, 0].astype(BF16)
        w_scr[:, tn:] = wl_ref[0, 0].astype(BF16)

    def compute(words):
        lo, hi = _unpack_rows(words)
        u = (jnp.dot(lo, w_scr[:PACK_W, :], preferred_element_type=F32)
             + jnp.dot(hi, w_scr[PACK_W:, :], preferred_element_type=F32))
        glu = jnp.minimum(u[:, :tn] + bg_ref[0, 0], SWIGLU_LIMIT)
        lin = jnp.clip(u[:, tn:] + bl_ref[0, 0], -SWIGLU_LIMIT, SWIGLU_LIMIT)
        return (glu * _sigmoid(SWIGLU_ALPHA * glu) * (lin + 1.0)).astype(BF16)

    _expert_tiles(ts_ref, nt_ref, xs_ref, act_ref, tn, in_buf, out_buf, sem_in, sem_out, compute)


def _gmm2_body(ts_ref, nt_ref, act_ref, w_ref, b_ref, yo_ref, in_buf, out_buf, w_scr, sem_in, sem_out):
    @pl.when(nt_ref[pl.program_id(1)] > 0)
    def _():
        w_scr[...] = w_ref[0, 0].astype(BF16)

    def compute(a):
        return jnp.dot(a, w_scr[...], preferred_element_type=F32) + b_ref[0, 0]

    _expert_tiles(ts_ref, nt_ref, act_ref, yo_ref, MOE_TN2, in_buf, out_buf, sem_in, sem_out, compute)


def _gmm_scratch(in_shape, in_dtype, tn, out_dtype):
    nbuf = MOE_LOOKAHEAD + 1
    return [pltpu.VMEM((nbuf,) + in_shape, in_dtype), pltpu.VMEM((2, MOE_TM, tn), out_dtype),
            pltpu.SemaphoreType.DMA((nbuf,)), pltpu.SemaphoreType.DMA((2,))]


def moe_gmm1(layer, xs, w1, b1, tile_start, tile_count):
    tn = MOE_TN1
    nb = D_FF // tn
    d = D_MODEL
    any_spec = pl.BlockSpec(memory_space=pl.ANY)
    in_buf, out_buf, sem_in, sem_out = _gmm_scratch((MOE_TM, PACK_W), jnp.uint32, tn, BF16)
    grid_spec = pltpu.PrefetchScalarGridSpec(
        num_scalar_prefetch=2,
        grid=(nb, N_EXPERTS),
        in_specs=[
            any_spec,
            pl.BlockSpec((1, 1, d, tn), lambda n, e, ts, nt: (layer, e, 0, n)),
            pl.BlockSpec((1, 1, d, tn), lambda n, e, ts, nt: (layer, e, 0, n + nb)),
            pl.BlockSpec((1, 1, 1, tn), lambda n, e, ts, nt: (layer, e, 0, n)),
            pl.BlockSpec((1, 1, 1, tn), lambda n, e, ts, nt: (layer, e, 0, n + nb)),
        ],
        out_specs=any_spec,
        scratch_shapes=[in_buf, out_buf, pltpu.VMEM((d, 2 * tn), BF16), sem_in, sem_out],
    )
    b1r = b1.reshape(DEPTH, N_EXPERTS, 1, 2 * D_FF)
    return pl.pallas_call(
        _gmm1_body,
        grid_spec=grid_spec,
        out_shape=jax.ShapeDtypeStruct((MOE_ROWS, D_FF), BF16),
        compiler_params=_cparams(("arbitrary", "arbitrary")),
        name="moe_gmm1",
    )(tile_start, tile_count, xs, w1, w1, b1r, b1r)


def moe_gmm2(layer, act, w2, b2, tile_start, tile_count):
    tn = MOE_TN2
    nb = D_MODEL // tn
    any_spec = pl.BlockSpec(memory_space=pl.ANY)
    in_buf, out_buf, sem_in, sem_out = _gmm_scratch((MOE_TM, D_FF), BF16, tn, F32)
    grid_spec = pltpu.PrefetchScalarGridSpec(
        num_scalar_prefetch=2,
        grid=(nb, N_EXPERTS),
        in_specs=[
            any_spec,
            pl.BlockSpec((1, 1, D_FF, tn), lambda n, e, ts, nt: (layer, e, 0, n)),
            pl.BlockSpec((1, 1, 1, tn), lambda n, e, ts, nt: (layer, e, 0, n)),
        ],
        out_specs=any_spec,
        scratch_shapes=[in_buf, out_buf, pltpu.VMEM((D_FF, tn), BF16), sem_in, sem_out],
    )
    return pl.pallas_call(
        _gmm2_body,
        grid_spec=grid_spec,
        out_shape=jax.ShapeDtypeStruct((MOE_ROWS, D_MODEL), F32),
        compiler_params=_cparams(("arbitrary", "arbitrary")),
        name="moe_gmm2",
    )(tile_start, tile_count, act, w2, b2.reshape(DEPTH, N_EXPERTS, 1, D_MODEL))


DISP_CHUNK = 256
N_SLOTS = N_ALL * TOP_K


def _dispatch_body(pos_ref, hp_ref, xs_in_ref, xs_ref, sem):
    del xs_in_ref
    base = pl.program_id(0) * (DISP_CHUNK * TOP_K)

    def row_copy(i, k):
        return pltpu.make_async_copy(hp_ref.at[pl.ds(i, 1)], xs_ref.at[pl.ds(pos_ref[base + i * TOP_K + k], 1)], sem)

    def all_rows(start):
        def body(i, carry):
            for k in range(TOP_K):
                cp = row_copy(i, k)
                cp.start(priority=k % 2) if start else cp.wait()
            return carry
        lax.fori_loop(0, DISP_CHUNK, body, 0, unroll=4)

    all_rows(True)
    all_rows(False)


def moe_dispatch(pos, hp):
    any_spec = pl.BlockSpec(memory_space=pl.ANY)
    grid_spec = pltpu.PrefetchScalarGridSpec(
        num_scalar_prefetch=1,
        grid=(N_ALL // DISP_CHUNK,),
        in_specs=[pl.BlockSpec((DISP_CHUNK, PACK_W), lambda i, pos: (i, 0)), any_spec],
        out_specs=any_spec,
        scratch_shapes=[pltpu.SemaphoreType.DMA(())],
    )
    xs0 = jnp.zeros((MOE_ROWS, PACK_W), jnp.uint32)
    return pl.pallas_call(
        _dispatch_body,
        grid_spec=grid_spec,
        out_shape=jax.ShapeDtypeStruct((MOE_ROWS, PACK_W), jnp.uint32),
        input_output_aliases={2: 0},
        compiler_params=_cparams(("arbitrary",)),
        name="moe_dispatch",
    )(pos, hp, xs0)


COMB_TB = 128


def _combine_body(final, pos_ref, yo_ref, x1_ref, tg_ref, nf_ref, o_ref, buf, sem):
    s = pl.program_id(0)
    ns = pl.num_programs(0)

    def row_copy(step, i, k, slot):
        j = (step * COMB_TB + i) * TOP_K + k
        return pltpu.make_async_copy(yo_ref.at[pl.ds(pos_ref[j], 1)], buf.at[slot, k, pl.ds(i, 1)], sem.at[slot])

    def start_tile(step, slot):
        def body(i, carry):
            for k in range(TOP_K):
                row_copy(step, i, k, slot).start(priority=k % 2)
            return carry
        lax.fori_loop(0, COMB_TB, body, 0, unroll=4)

    def wait_tile(slot):
        for k in range(TOP_K):
            pltpu.make_async_copy(yo_ref.at[pl.ds(0, COMB_TB)], buf.at[slot, k], sem.at[slot]).wait()

    @pl.when(s == 0)
    def _():
        start_tile(0, 0)

    @pl.when(s + 1 < ns)
    def _():
        start_tile(s + 1, (s + 1) % 2)

    slot = s % 2
    wait_tile(slot)
    acc = x1_ref[...]
    for k in range(TOP_K):
        acc = acc + tg_ref[:, k:k + 1] * buf[slot, k]
    o_ref[...] = _rms(acc, nf_ref[...]) if final else acc


def moe_combine(pos, yo, x1, tg, nf, final):
    tb = COMB_TB
    grid_spec = pltpu.PrefetchScalarGridSpec(
        num_scalar_prefetch=1,
        grid=(N_ALL // tb,),
        in_specs=[pl.BlockSpec(memory_space=pl.ANY),
                  pl.BlockSpec((tb, D_MODEL), lambda i, pos: (i, 0)),
                  pl.BlockSpec((tb, ROUTE_LANES), lambda i, pos: (i, 0)),
                  pl.BlockSpec((1, D_MODEL), lambda i, pos: (0, 0))],
        out_specs=pl.BlockSpec((tb, D_MODEL), lambda i, pos: (i, 0)),
        scratch_shapes=[pltpu.VMEM((2, TOP_K, tb, D_MODEL), F32), pltpu.SemaphoreType.DMA((2,))],
    )
    return pl.pallas_call(
        functools.partial(_combine_body, final),
        grid_spec=grid_spec,
        out_shape=jax.ShapeDtypeStruct((N_ALL, D_MODEL), F32),
        compiler_params=_cparams(("arbitrary",)),
        name="moe_combine",
    )(pos, yo, x1, tg, _row(nf))


def moe_layout(ti):
    flat_e = ti.reshape(-1)
    oh = (flat_e[:, None] == jnp.arange(N_EXPERTS, dtype=jnp.int32)[None, :]).astype(jnp.int32)
    csum = jnp.cumsum(oh, axis=0)
    counts = csum[-1]
    rank = jnp.sum((csum - oh) * oh, axis=1)
    tile_count = (counts + MOE_TM - 1) // MOE_TM
    tile_start = jnp.cumsum(tile_count) - tile_count
    pos = (tile_start * MOE_TM)[flat_e] + rank
    return pos.astype(jnp.int32), tile_start.astype(jnp.int32), tile_count.astype(jnp.int32)


def _block_diag(w):
    eye = jnp.eye(HB_BLOCKS, dtype=w.dtype)
    return jnp.einsum("hij,hg->higj", w, eye).reshape(HB_WIDTH, HB_WIDTH)


def kernel(x_prompt, x_sample, mem_prompt, state_hgrn, state_lru, state_conv, cache_mem_k, cache_mem_v,
           norm_mix, w_in, hgrn_lb, hgrn_onorm, conv_w, conv_b, lru_wa, lru_ba, lru_wx, lru_bx, lru_L,
           norm_lru, norm_xattn, norm_mem, w_mem_k, w_mem_v, w_out, norm_ffn, router_w, router_b,
           moe_w1, moe_b1, moe_w2, moe_b2, norm_final):
    x = jnp.concatenate([x_prompt.reshape(N_P, D_MODEL), x_sample.reshape(N_S, D_MODEL)], axis=0)
    mem = mem_prompt.reshape(BATCH * N_MEM, D_MODEL)
    cache_k = cache_mem_k.reshape(DEPTH, DEC_BATCH, N_MEM * HC_HEADS, HC_DH)
    cache_v = cache_mem_v.reshape(DEPTH, DEC_BATCH, N_MEM * HC_HEADS, HC_DH)
    conv_s = state_conv.reshape(DEPTH, DEC_BATCH, (CONV_W - 1) * HB_WIDTH)
    hgrn_s_out = jnp.zeros(state_hgrn.shape, F32)

    p_hgrn, p_lru, p_conv, p_mk, p_mv, s_lru, s_conv = [], [], [], [], [], [], []
    for l in range(DEPTH):
        w_mem = jnp.concatenate([w_mem_k[l], w_mem_v[l]], axis=1).astype(BF16)
        memkv = norm_matmul(mem, norm_mem[l], w_mem, tm=BATCH * N_MEM, tn=512)
        p_mk.append(memkv[:, :HC_WIDTH].reshape(BATCH, N_MEM, HC_HEADS, HC_DH))
        p_mv.append(memkv[:, HC_WIDTH:].reshape(BATCH, N_MEM, HC_HEADS, HC_DH))

        u = norm_matmul(x, norm_mix[l], w_in[l].astype(BF16), tm=N_ALL // 8, tn=512)

        oa_p, hg_p = hgrn_prompt(l, u, hgrn_lb, hgrn_onorm[l])
        oa_s, hgrn_s_out = hgrn_sample(l, u, hgrn_lb, hgrn_onorm[l], state_hgrn, hgrn_s_out)

        wa = _block_diag(lru_wa[l]).astype(BF16)
        wx = _block_diag(lru_wx[l]).astype(BF16)
        lru_args = (conv_w[l], conv_b[l], wa, wx, lru_ba[l], lru_bx[l], lru_L[l], norm_lru[l])
        ob_p, hl_p, cs_p = lru_prompt(u, *lru_args)
        ob_s, hl_s, cs_s = lru_sample(l, u, state_lru, conv_s, *lru_args)

        oc_p = xattn_prompt(u, memkv, norm_xattn[l])
        oc_s = xattn_sample(l, u, cache_k, cache_v, norm_xattn[l])

        x1, hp, ti, tg = post_mixer((oa_p, ob_p, oc_p), (oa_s, ob_s, oc_s), x, w_out[l].astype(BF16),
                                    norm_ffn[l], router_w[l], router_b[l])
        pos, tile_start, tile_count = moe_layout(ti[:, :TOP_K])
        xs = moe_dispatch(pos, hp)
        act = moe_gmm1(l, xs, moe_w1, moe_b1, tile_start, tile_count)
        yo = moe_gmm2(l, act, moe_w2, moe_b2, tile_start, tile_count)
        x = moe_combine(pos, yo, x1, tg, norm_final, final=(l == DEPTH - 1))

        p_hgrn.append(hg_p)
        p_lru.append(hl_p.reshape(BATCH, HB_WIDTH))
        p_conv.append(cs_p)
        s_lru.append(hl_s)
        s_conv.append(cs_s.reshape(DEC_BATCH, CONV_W - 1, HB_WIDTH))

    y_prompt = x[:N_P].reshape(BATCH, SEQ, D_MODEL)
    y_sample = x[N_P:].reshape(DEC_BATCH, DEC_SEQ, D_MODEL)
    return (y_prompt, y_sample, jnp.stack(p_hgrn), jnp.stack(p_lru), jnp.stack(p_conv),
            jnp.stack(p_mk), jnp.stack(p_mv), hgrn_s_out, jnp.stack(s_lru), jnp.stack(s_conv))
```

```python
import functools
import math

import jax
import jax.numpy as jnp
from jax import lax
from jax.experimental import pallas as pl
from jax.experimental.pallas import tpu as pltpu

F32 = jnp.float32
BF16 = jnp.bfloat16

D_MODEL = 2048
BATCH = 4
SEQ = 2048
DEPTH = 2
DEC_BATCH = 128
DEC_SEQ = 4
HA_DK = 128
HA_WIDTH = D_MODEL // 2
HA_HEADS = HA_WIDTH // HA_DK
HA_DV = HA_WIDTH // HA_HEADS
HB_WIDTH = D_MODEL // 4
HB_BLOCKS = 8
HB_BW = HB_WIDTH // HB_BLOCKS
CONV_W = 4
LRU_C = 8.0
HC_HEADS = 4
HC_WIDTH = D_MODEL - HA_WIDTH - HB_WIDTH
HC_DH = HC_WIDTH // HC_HEADS
N_MEM = 256
IN_COLS = 4 * HA_WIDTH + 2 * HB_WIDTH + HC_WIDTH
N_EXPERTS = 32
TOP_K = 4
D_FF = D_MODEL
SWIGLU_LIMIT = 7.0
SWIGLU_ALPHA = 1.702
EPS = 1e-6

N_P = BATCH * SEQ
N_S = DEC_BATCH * DEC_SEQ
N_ALL = N_P + N_S

COL_QA, COL_FA, COL_IA, COL_GA = 0, 1, 2, 3
COL_XB, COL_GB, COL_QC = 8, 9, 10

HGRN_CHUNK = 64
HGRN_SUB = 16
LRU_TB = 256
XATTN_TQ = 512
MOE_TM = 256
MOE_LOOKAHEAD = 3
TILE_DMA_PRIORITY = 0
MOE_TN1 = 1024
MOE_TN2 = 2048
MOE_TILES = (N_ALL * TOP_K) // MOE_TM + N_EXPERTS
MOE_ROWS = MOE_TILES * MOE_TM
VMEM_LIMIT = 56 * 1024 * 1024

NT_DIMS = (((1,), (1,)), ((), ()))
TN_DIMS = (((0,), (0,)), ((), ()))


def _cparams(sem):
    return pltpu.CompilerParams(dimension_semantics=sem, vmem_limit_bytes=VMEM_LIMIT)


def _rms(x, g):
    return x * lax.rsqrt(jnp.mean(x * x, axis=-1, keepdims=True) + EPS) * g


def _sigmoid(x):
    return jax.nn.sigmoid(x)


def _silu(x):
    return x * jax.nn.sigmoid(x)


def _norm_matmul_body(x_ref, g_ref, w_ref, o_ref, xn_ref):
    @pl.when(pl.program_id(1) == 0)
    def _():
        xn_ref[...] = _rms(x_ref[...], g_ref[...]).astype(BF16)

    o_ref[...] = jnp.dot(xn_ref[...], w_ref[...], preferred_element_type=F32)


def norm_matmul(x, g, w, tm, tn):
    n, d = x.shape
    nc = w.shape[1]
    return pl.pallas_call(
        _norm_matmul_body,
        grid=(n // tm, nc // tn),
        in_specs=[
            pl.BlockSpec((tm, d), lambda i, j: (i, 0)),
            pl.BlockSpec((1, d), lambda i, j: (0, 0)),
            pl.BlockSpec((d, tn), lambda i, j: (0, j)),
        ],
        out_specs=pl.BlockSpec((tm, tn), lambda i, j: (i, j)),
        out_shape=jax.ShapeDtypeStruct((n, nc), F32),
        scratch_shapes=[pltpu.VMEM((tm, d), BF16)],
        compiler_params=_cparams(("parallel", "arbitrary")),
        name="norm_matmul",
    )(x, g.reshape(1, d), w)


def _hgrn_lower_bound(lbp, layer):
    rows = [lbp[r:r + 1, :] for r in range(DEPTH)]
    m = functools.reduce(jnp.maximum, rows)
    es = [jnp.exp(r - m) for r in rows]
    tot = functools.reduce(lambda a, b: a + b, es)
    lb = jnp.zeros_like(m)
    for r in range(1, layer + 1):
        lb = lb + es[r] / tot
    return lb


def _hgrn_gates(z, qa, lb):
    f = lb + (1.0 - lb) * _sigmoid(z)
    k = (1.0 - lb) * _sigmoid(-z)
    q = _silu(qa)
    return f, k, q


def _hgrn_prompt_body(layer, qa_ref, fa_ref, ia_ref, ga_ref, lbp_ref, on_ref, oa_ref, st_ref, s_scr):
    c = pl.program_id(1)
    C, SUB = HGRN_CHUNK, HGRN_SUB
    nsub = C // SUB

    @pl.when(c == 0)
    def _():
        s_scr[...] = jnp.zeros_like(s_scr)

    lb = _hgrn_lower_bound(lbp_ref[...], layer)
    f, k, q = _hgrn_gates(fa_ref[...], qa_ref[...], lb)
    g = jnp.log(f)
    v = ia_ref[...]
    gate = _silu(ga_ref[...])

    row = lax.broadcasted_iota(jnp.int32, (C, C), 0)
    col = lax.broadcasted_iota(jnp.int32, (C, C), 1)
    tri = (row >= col).astype(F32)
    b_all = jnp.dot(tri, g, preferred_element_type=F32, precision=lax.Precision.HIGHEST)

    row_sub = lax.broadcasted_iota(jnp.int32, (SUB, HA_DK), 0)
    row_c = lax.broadcasted_iota(jnp.int32, (C, HA_DK), 0)
    lane_c = lax.broadcasted_iota(jnp.int32, (SUB, C), 1)
    neg_inf = jnp.float32(-jnp.inf)

    states = [s_scr[h] for h in range(HA_HEADS)]
    new_states, outs = [], []
    for h in range(HA_HEADS):
        sl = slice(h * HA_DK, (h + 1) * HA_DK)
        bh, qh, kh, vh = b_all[:, sl], q[:, sl], k[:, sl], v[:, sl]
        kh16 = kh.astype(BF16)
        vh16 = vh.astype(BF16)
        b_end = bh[C - 1:C, :]
        st = states[h]
        o = lax.dot_general((qh * jnp.exp(bh)).astype(BF16), st.astype(BF16), NT_DIMS,
                            preferred_element_type=F32)
        kdec = (kh * jnp.exp(b_end - bh)).astype(BF16)
        upd = lax.dot_general(vh16, kdec, TN_DIMS, preferred_element_type=F32)
        new_states.append(st * jnp.exp(b_end) + upd)

        a_rows = []
        for i in range(nsub):
            bi = bh[i * SUB:(i + 1) * SUB, :]
            qi = qh[i * SUB:(i + 1) * SUB, :]
            parts = []
            for s in range(SUB):
                dec = jnp.exp(jnp.where(row_sub >= s, bi - bi[s:s + 1, :], neg_inf))
                parts.append(qi * dec)
            q_all = jnp.concatenate(parts, axis=0).astype(BF16)
            m = lax.dot_general(q_all, kh16, NT_DIMS, preferred_element_type=F32)
            a_i = jnp.zeros((SUB, C), F32)
            for s in range(SUB):
                a_i = a_i + jnp.where(lane_c == i * SUB + s, m[s * SUB:(s + 1) * SUB, :], 0.0)
            if i > 0:
                b_prev = bh[i * SUB - 1:i * SUB, :]
                q_i = (qi * jnp.exp(bi - b_prev)).astype(BF16)
                k_i = (kh * jnp.exp(jnp.where(row_c < i * SUB, b_prev - bh, neg_inf))).astype(BF16)
                a_i = a_i + lax.dot_general(q_i, k_i, NT_DIMS, preferred_element_type=F32)
            a_rows.append(a_i)
        att = jnp.concatenate(a_rows, axis=0).astype(BF16)
        o = o + jnp.dot(att, vh16, preferred_element_type=F32)
        outs.append(_rms(o, on_ref[...]) * gate[:, sl])

    for h in range(HA_HEADS):
        oa_ref[:, h * HA_DK:(h + 1) * HA_DK] = outs[h]
        s_scr[h] = new_states[h]

    @pl.when(c == pl.num_programs(1) - 1)
    def _():
        for h in range(HA_HEADS):
            st_ref[0, h] = new_states[h].T


def hgrn_prompt(layer, u, lbp, onorm):
    nchunk = SEQ // HGRN_CHUNK

    def spec(colblk):
        return pl.BlockSpec((HGRN_CHUNK, HA_WIDTH), lambda b, c: (b * nchunk + c, colblk))

    return pl.pallas_call(
        functools.partial(_hgrn_prompt_body, layer),
        grid=(BATCH, nchunk),
        in_specs=[spec(COL_QA), spec(COL_FA), spec(COL_IA), spec(COL_GA),
                  pl.BlockSpec((DEPTH, HA_WIDTH), lambda b, c: (0, 0)),
                  pl.BlockSpec((1, HA_DV), lambda b, c: (0, 0))],
        out_specs=[pl.BlockSpec((HGRN_CHUNK, HA_WIDTH), lambda b, c: (b * nchunk + c, 0)),
                   pl.BlockSpec((1, HA_HEADS, HA_DK, HA_DV), lambda b, c: (b, 0, 0, 0))],
        out_shape=[jax.ShapeDtypeStruct((N_P, HA_WIDTH), F32),
                   jax.ShapeDtypeStruct((BATCH, HA_HEADS, HA_DK, HA_DV), F32)],
        scratch_shapes=[pltpu.VMEM((HA_HEADS, HA_DV, HA_DK), F32)],
        compiler_params=_cparams(("parallel", "arbitrary")),
        name="hgrn_prompt",
    )(u, u, u, u, lbp, onorm.reshape(1, HA_DV))


HGRN_SB = 2


def _hgrn_sample_body(layer, qa_ref, fa_ref, ia_ref, ga_ref, lbp_ref, on_ref, s_ref, so_in_ref, oa_ref, so_ref):
    del so_in_ref
    assert HGRN_SB == 2 and DEC_SEQ >= 3
    R = HGRN_SB * DEC_SEQ
    lb = _hgrn_lower_bound(lbp_ref[...], layer)
    f, k, q = _hgrn_gates(fa_ref[...], qa_ref[...], lb)
    g = jnp.log(f)
    v = ia_ref[...]
    gate = _silu(ga_ref[...])

    ri = lax.broadcasted_iota(jnp.int32, (R, R), 0)
    ci = lax.broadcasted_iota(jnp.int32, (R, R), 1)
    same_elem = (ri >= DEC_SEQ) == (ci >= DEC_SEQ)
    tri = (same_elem & (ri >= ci)).astype(F32)
    b = jnp.dot(tri, g, preferred_element_type=F32, precision=lax.Precision.HIGHEST)
    row_w = lax.broadcasted_iota(jnp.int32, b.shape, 0)
    b_end = jnp.where(row_w < DEC_SEQ, b[DEC_SEQ - 1:DEC_SEQ, :], b[R - 1:R, :])
    qe = q * jnp.exp(b)
    kdec = k * jnp.exp(b_end - b)
    e_end = jnp.exp(b_end)

    row = lax.broadcasted_iota(jnp.int32, (R, HA_DK), 0)
    elem = (row >= DEC_SEQ).astype(jnp.int32)
    trow = row - elem * DEC_SEQ
    lane_r = lax.broadcasted_iota(jnp.int32, (R, R), 1)
    neg_inf = jnp.float32(-jnp.inf)

    states = [[s_ref[0, bb, h] for h in range(HA_HEADS)] for bb in range(HGRN_SB)]
    new_states = [[None] * HA_HEADS for _ in range(HGRN_SB)]
    outs = []
    for h in range(HA_HEADS):
        sl = slice(h * HA_DK, (h + 1) * HA_DK)
        bh, qh, kh, vh = b[:, sl], q[:, sl], k[:, sl], v[:, sl]
        vh16 = vh.astype(BF16)
        parts = []
        for s in range(R):
            valid = (elem == s // DEC_SEQ) & (row >= s)
            parts.append(qh * jnp.exp(jnp.where(valid, bh - bh[s:s + 1, :], neg_inf)))
        q_all = jnp.concatenate(parts, axis=0).astype(BF16)
        m = lax.dot_general(q_all, kh.astype(BF16), NT_DIMS, preferred_element_type=F32)
        att = jnp.zeros((R, R), F32)
        for s in range(R):
            att = att + jnp.where(lane_r == s, m[s * R:(s + 1) * R, :], 0.0)
        o = jnp.dot(att.astype(BF16), vh16, preferred_element_type=F32)

        qe16 = qe[:, sl].astype(BF16)
        for bb in range(HGRN_SB):
            s0 = states[bb][h]
            o = o + jnp.where(elem == bb, jnp.dot(qe16, s0.astype(BF16), preferred_element_type=F32), 0.0)
            e = e_end[bb * DEC_SEQ:bb * DEC_SEQ + 1, sl]
            e1 = e.astype(BF16).astype(F32)
            e2 = (e - e1).astype(BF16).astype(F32)
            e3 = e - e1 - e2
            eparts = jnp.where(trow == 0, e1, jnp.where(trow == 1, e2, jnp.where(trow == 2, e3, 0.0)))
            own = elem == bb
            lhs = jnp.where(own, kdec[:, sl], eparts).astype(BF16)
            ones = jnp.where(own | (trow > 2), 0.0, 1.0)
            rhs = jnp.concatenate([ones, jnp.where(own, vh, 0.0)], axis=-1).astype(BF16)
            res = lax.dot_general(lhs, rhs, TN_DIMS, preferred_element_type=F32)
            new_states[bb][h] = res[:, :HA_DV] * s0 + res[:, HA_DV:]
        outs.append(_rms(o, on_ref[...]) * gate[:, sl])

    for h in range(HA_HEADS):
        oa_ref[:, h * HA_DK:(h + 1) * HA_DK] = outs[h]
        for bb in range(HGRN_SB):
            so_ref[0, bb, h] = new_states[bb][h]


def hgrn_sample(layer, u, lbp, onorm, state, so_buf):
    rows8 = HGRN_SB * DEC_SEQ
    row0 = N_P // rows8

    def spec(colblk):
        return pl.BlockSpec((rows8, HA_WIDTH), lambda i: (row0 + i, colblk))

    st_spec = pl.BlockSpec((1, HGRN_SB, HA_HEADS, HA_DK, HA_DV), lambda i: (layer, i, 0, 0, 0))
    return pl.pallas_call(
        functools.partial(_hgrn_sample_body, layer),
        grid=(DEC_BATCH // HGRN_SB,),
        in_specs=[spec(COL_QA), spec(COL_FA), spec(COL_IA), spec(COL_GA),
                  pl.BlockSpec((DEPTH, HA_WIDTH), lambda i: (0, 0)),
                  pl.BlockSpec((1, HA_DV), lambda i: (0, 0)),
                  st_spec, pl.BlockSpec(memory_space=pl.ANY)],
        out_specs=[pl.BlockSpec((rows8, HA_WIDTH), lambda i: (i, 0)), st_spec],
        out_shape=[jax.ShapeDtypeStruct((N_S, HA_WIDTH), F32), jax.ShapeDtypeStruct(state.shape, F32)],
        input_output_aliases={7: 1},
        compiler_params=_cparams(("parallel",)),
        name="hgrn_sample",
    )(u, u, u, u, lbp, onorm.reshape(1, HA_DV), state, so_buf)


def _softplus(x):
    return jnp.maximum(x, 0.0) + jnp.log1p(jnp.exp(-jnp.abs(x)))


def _lru_gates(xc, wa_ref, wx_ref, ba, bx, sp):
    xc16 = xc.astype(BF16)
    r = _sigmoid(jnp.dot(xc16, wa_ref[...], preferred_element_type=F32) + ba)
    i = _sigmoid(jnp.dot(xc16, wx_ref[...], preferred_element_type=F32) + bx)
    log_a = -LRU_C * r * sp
    a = jnp.exp(log_a)
    th = jnp.tanh(log_a)
    mult = jnp.sqrt(2.0 * th / (th - 1.0))
    return a, mult * i * xc


def _lru_prompt_body(xb_ref, gb_ref, cw_ref, cb_ref, wa_ref, wx_ref, ba_ref, bx_ref, l_ref, nl_ref,
                     ob_ref, hl_ref, cs_ref, xe_scr, h_scr):
    t = pl.program_id(1)
    tb = LRU_TB

    @pl.when(t == 0)
    def _():
        xe_scr[0:8, :] = jnp.zeros((8, HB_WIDTH), F32)
        h_scr[...] = jnp.zeros_like(h_scr)

    xb = xb_ref[...]
    xe_scr[8:8 + tb, :] = xb
    cw = cw_ref[...]
    xc = cb_ref[...] + cw[3:4, :] * xb
    for j in range(1, CONV_W):
        xc = xc + cw[3 - j:4 - j, :] * xe_scr[pl.ds(8 - j, tb), :]
    xe_scr[0:8, :] = xb[tb - 8:tb, :]

    sp = _softplus(-l_ref[...])
    a, bt = _lru_gates(xc, wa_ref, wx_ref, ba_ref[...], bx_ref[...], sp)
    row = lax.broadcasted_iota(jnp.int32, (tb, HB_WIDTH), 0)
    sh = 1
    while sh < tb:
        keep = row >= sh
        a_sh = jnp.where(keep, pltpu.roll(a, sh, 0), 1.0)
        b_sh = jnp.where(keep, pltpu.roll(bt, sh, 0), 0.0)
        bt = a * b_sh + bt
        a = a * a_sh
        sh *= 2
    hcur = bt + a * h_scr[0:1, :]
    h_last = hcur[tb - 1:tb, :]
    h_scr[...] = jnp.broadcast_to(h_last, h_scr.shape)
    y = hcur * jax.nn.gelu(gb_ref[...])
    ob_ref[...] = _rms(y, nl_ref[...])
    hl_ref[0] = h_last
    cs_ref[0] = xb[tb - (CONV_W - 1):tb, :]


def _row(p):
    return p.reshape(1, -1)


def lru_prompt(u, cw, cb, wa, wx, ba, bx, lam, nl):
    nt = SEQ // LRU_TB
    w = HB_WIDTH

    def uspec(colblk):
        return pl.BlockSpec((LRU_TB, w), lambda b, t: (b * nt + t, colblk))

    def full(shape):
        return pl.BlockSpec(shape, lambda b, t: (0,) * len(shape))

    return pl.pallas_call(
        _lru_prompt_body,
        grid=(BATCH, nt),
        in_specs=[uspec(COL_XB), uspec(COL_GB), full((CONV_W, w)), full((1, w)), full((w, w)), full((w, w)),
                  full((1, w)), full((1, w)), full((1, w)), full((1, w))],
        out_specs=[pl.BlockSpec((LRU_TB, w), lambda b, t: (b * nt + t, 0)),
                   pl.BlockSpec((1, 1, w), lambda b, t: (b, 0, 0)),
                   pl.BlockSpec((1, CONV_W - 1, w), lambda b, t: (b, 0, 0))],
        out_shape=[jax.ShapeDtypeStruct((N_P, w), F32),
                   jax.ShapeDtypeStruct((BATCH, 1, w), F32),
                   jax.ShapeDtypeStruct((BATCH, CONV_W - 1, w), F32)],
        scratch_shapes=[pltpu.VMEM((LRU_TB + 8, w), F32), pltpu.VMEM((8, w), F32)],
        compiler_params=_cparams(("parallel", "arbitrary")),
        name="lru_prompt",
    )(u, u, cw, _row(cb), wa, wx, _row(ba), _row(bx), _row(lam), _row(nl))


def _lru_sample_body(xb_ref, gb_ref, h0_ref, cv_ref, cw_ref, cb_ref, wa_ref, wx_ref, ba_ref, bx_ref, l_ref, nl_ref,
                     ob_ref, hl_ref, cs_ref, x_scr, g_scr, y_scr):
    w = HB_WIDTH
    nb = DEC_BATCH
    nchunk = w // 128
    cw = cw_ref[...]
    for c in range(nchunk):
        x_scr[c] = xb_ref[:, c * 128:(c + 1) * 128]
        g_scr[c] = gb_ref[:, c * 128:(c + 1) * 128]

    def time_rows(scr, t):
        return jnp.concatenate([scr[c, pl.ds(t, nb, stride=DEC_SEQ), :] for c in range(nchunk)], axis=-1)

    xs = [cv_ref[0, :, j * w:(j + 1) * w] for j in range(CONV_W - 1)]
    xs += [time_rows(x_scr, t) for t in range(DEC_SEQ)]
    sp = _softplus(-l_ref[...])
    hcur = h0_ref[0]
    for t in range(DEC_SEQ):
        xc = cb_ref[...]
        for j in range(CONV_W):
            xc = xc + cw[j:j + 1, :] * xs[t + j]
        a, bt = _lru_gates(xc, wa_ref, wx_ref, ba_ref[...], bx_ref[...], sp)
        hcur = a * hcur + bt
        y = _rms(hcur * jax.nn.gelu(time_rows(g_scr, t)), nl_ref[...])
        for c in range(nchunk):
            y_scr[c, pl.ds(t, nb, stride=DEC_SEQ), :] = y[:, c * 128:(c + 1) * 128]
    for c in range(nchunk):
        ob_ref[:, c * 128:(c + 1) * 128] = y_scr[c]
    hl_ref[...] = hcur
    for j in range(CONV_W - 1):
        cs_ref[:, j * w:(j + 1) * w] = xs[DEC_SEQ + j]


def lru_sample(layer, u, h0, conv, cw, cb, wa, wx, ba, bx, lam, nl):
    w = HB_WIDTH
    row0 = N_P // N_S

    def full(shape):
        return pl.BlockSpec(shape, lambda i: (0,) * len(shape))

    return pl.pallas_call(
        _lru_sample_body,
        grid=(1,),
        in_specs=[pl.BlockSpec((N_S, w), lambda i: (row0, COL_XB)),
                  pl.BlockSpec((N_S, w), lambda i: (row0, COL_GB)),
                  pl.BlockSpec((1, DEC_BATCH, w), lambda i: (layer, 0, 0)),
                  pl.BlockSpec((1, DEC_BATCH, (CONV_W - 1) * w), lambda i: (layer, 0, 0)),
                  full((CONV_W, w)), full((1, w)), full((w, w)), full((w, w)),
                  full((1, w)), full((1, w)), full((1, w)), full((1, w))],
        out_specs=[full((N_S, w)), full((DEC_BATCH, w)), full((DEC_BATCH, (CONV_W - 1) * w))],
        out_shape=[jax.ShapeDtypeStruct((N_S, w), F32),
                   jax.ShapeDtypeStruct((DEC_BATCH, w), F32),
                   jax.ShapeDtypeStruct((DEC_BATCH, (CONV_W - 1) * w), F32)],
        scratch_shapes=[pltpu.VMEM((w // 128, N_S, 128), F32)] * 3,
        compiler_params=_cparams(("arbitrary",)),
        name="lru_sample",
    )(u, u, h0, conv, cw, _row(cb), wa, wx, _row(ba), _row(bx), _row(lam), _row(nl))


XATTN_SCALE = 1.0 / math.sqrt(HC_DH)


def _xattn_prompt_body(q_ref, k_ref, v_ref, nx_ref, oc_ref):
    q = q_ref[...]
    outs = []
    for h in range(HC_HEADS):
        sl = slice(h * HC_DH, (h + 1) * HC_DH)
        s = lax.dot_general(q[:, sl].astype(BF16), k_ref[0, :, sl].astype(BF16), NT_DIMS,
                            preferred_element_type=F32) * XATTN_SCALE
        p = jnp.exp(s - jnp.max(s, axis=-1, keepdims=True))
        p = p / jnp.sum(p, axis=-1, keepdims=True)
        outs.append(jnp.dot(p.astype(BF16), v_ref[0, :, sl].astype(BF16), preferred_element_type=F32))
    oc_ref[...] = _rms(jnp.concatenate(outs, axis=-1), nx_ref[...])


def xattn_prompt(u, memkv, nx):
    nq = SEQ // XATTN_TQ
    w = HC_WIDTH
    kv = memkv.reshape(BATCH, N_MEM, 2 * w)
    return pl.pallas_call(
        _xattn_prompt_body,
        grid=(BATCH, nq),
        in_specs=[pl.BlockSpec((XATTN_TQ, w), lambda b, t: (b * nq + t, COL_QC)),
                  pl.BlockSpec((1, N_MEM, w), lambda b, t: (b, 0, 0)),
                  pl.BlockSpec((1, N_MEM, w), lambda b, t: (b, 0, 1)),
                  pl.BlockSpec((1, w), lambda b, t: (0, 0))],
        out_specs=pl.BlockSpec((XATTN_TQ, w), lambda b, t: (b * nq + t, 0)),
        out_shape=jax.ShapeDtypeStruct((N_P, w), F32),
        compiler_params=_cparams(("parallel", "parallel")),
        name="xattn_prompt",
    )(u, kv, kv, _row(nx))


XATTN_SB = 8


def _xattn_sample_body(q_ref, k_ref, v_ref, nx_ref, oc_ref):
    assert HC_HEADS == 4 and DEC_SEQ == 4
    q = q_ref[...]
    nrow = N_MEM * HC_HEADS
    ncol = HC_HEADS * 8
    row8 = lax.broadcasted_iota(jnp.int32, (8, HC_DH), 0)
    mem_head = lax.broadcasted_iota(jnp.int32, (nrow, ncol), 0) & (HC_HEADS - 1)
    col_head = lax.shift_right_logical(lax.broadcasted_iota(jnp.int32, (nrow, ncol), 1), 3)
    own_head = mem_head == col_head
    pair_rows = []
    for pair in range(XATTN_SB // 2):
        q_all = jnp.concatenate([q[pair * 8:(pair + 1) * 8, h * HC_DH:(h + 1) * HC_DH] for h in range(HC_HEADS)],
                                axis=0).astype(BF16)
        o_all = []
        for e in range(2):
            bb = pair * 2 + e
            st = lax.dot_general(k_ref[0, bb].astype(BF16), q_all, NT_DIMS, preferred_element_type=F32)
            st = jnp.where(own_head, st * XATTN_SCALE, -jnp.inf)
            pt = jnp.exp(st - jnp.max(st, axis=0, keepdims=True))
            pt = pt / jnp.sum(pt, axis=0, keepdims=True)
            o_all.append(lax.dot_general(pt.astype(BF16), v_ref[0, bb].astype(BF16), TN_DIMS,
                                         preferred_element_type=F32))
        head_cols = [jnp.where(row8 < DEC_SEQ, o_all[0][h * 8:(h + 1) * 8], o_all[1][h * 8:(h + 1) * 8])
                     for h in range(HC_HEADS)]
        pair_rows.append(jnp.concatenate(head_cols, axis=-1))
    oc_ref[...] = _rms(jnp.concatenate(pair_rows, axis=0), nx_ref[...])


def xattn_sample(layer, u, cache_k, cache_v, nx):
    w = HC_WIDTH
    rows = XATTN_SB * DEC_SEQ
    row0 = N_P // rows
    kv_spec = pl.BlockSpec((1, XATTN_SB, N_MEM * HC_HEADS, HC_DH), lambda i: (layer, i, 0, 0))
    return pl.pallas_call(
        _xattn_sample_body,
        grid=(DEC_BATCH // XATTN_SB,),
        in_specs=[pl.BlockSpec((rows, w), lambda i: (row0 + i, COL_QC)), kv_spec, kv_spec,
                  pl.BlockSpec((1, w), lambda i: (0, 0))],
        out_specs=pl.BlockSpec((rows, w), lambda i: (i, 0)),
        out_shape=jax.ShapeDtypeStruct((N_S, w), F32),
        compiler_params=_cparams(("parallel",)),
        name="xattn_sample",
    )(u, cache_k, cache_v, _row(nx))


POST_TM = 512
ROUTE_LANES = 128
PACK_W = D_MODEL // 2


def _post_body(oap_ref, obp_ref, ocp_ref, oas_ref, obs_ref, ocs_ref, x_ref, w_ref, nf_ref, rw_ref, rb_ref,
               x1_ref, hn_ref, ti_ref, tg_ref):
    is_sample = pl.program_id(0) == pl.num_programs(0) - 1

    def mixed(p_ref, s_ref):
        return jnp.where(is_sample, s_ref[...], p_ref[...]).astype(BF16)

    mix = jnp.concatenate([mixed(oap_ref, oas_ref), mixed(obp_ref, obs_ref), mixed(ocp_ref, ocs_ref)], axis=-1)
    x1 = x_ref[...] + jnp.dot(mix, w_ref[...], preferred_element_type=F32)
    x1_ref[...] = x1
    hn = _rms(x1, nf_ref[...])
    bits = lax.bitcast_convert_type(hn.astype(BF16).astype(F32), jnp.uint32)
    hn_ref[...] = (bits[:, :PACK_W] >> 16) | (bits[:, PACK_W:] & jnp.uint32(0xFFFF0000))
    hn_hi = hn.astype(BF16)
    hn_lo = (hn - hn_hi.astype(F32)).astype(BF16)
    rw = rw_ref[...]
    rw_hi = rw.astype(BF16)
    rw_lo = (rw - rw_hi.astype(F32)).astype(BF16)
    logits = (jnp.dot(hn_hi, rw_hi, preferred_element_type=F32) + jnp.dot(hn_hi, rw_lo, preferred_element_type=F32)
              + jnp.dot(hn_lo, rw_hi, preferred_element_type=F32) + rb_ref[...])
    tm = logits.shape[0]
    col = lax.broadcasted_iota(jnp.int32, (tm, N_EXPERTS), 1).astype(F32)
    lane = lax.broadcasted_iota(jnp.int32, (tm, ROUTE_LANES), 1)
    work = logits
    ti = jnp.zeros((tm, ROUTE_LANES), jnp.int32)
    tv = jnp.zeros((tm, ROUTE_LANES), F32)
    vals = []
    for kk in range(TOP_K):
        m = jnp.max(work, axis=-1, keepdims=True)
        idx = jnp.min(jnp.where(work == m, col, float(N_EXPERTS)), axis=-1, keepdims=True)
        work = jnp.where(col == idx, -jnp.inf, work)
        vals.append(m)
        ti = jnp.where(lane == kk, idx.astype(jnp.int32), ti)
    es = [jnp.exp(m - vals[0]) for m in vals]
    tot = functools.reduce(lambda a, b: a + b, es)
    for kk in range(TOP_K):
        tv = jnp.where(lane == kk, es[kk] / tot, tv)
    ti_ref[...] = ti
    tg_ref[...] = tv


def post_mixer(prompt_mix, sample_mix, x, w_out, nf, rw, rb):
    tm = POST_TM
    assert tm == N_S
    d = D_MODEL
    last_p = N_P // tm - 1

    def rows(width):
        return pl.BlockSpec((tm, width), lambda i: (i, 0))

    def prows(width):
        return pl.BlockSpec((tm, width), lambda i: (jnp.minimum(i, last_p), 0))

    def full(shape):
        return pl.BlockSpec(shape, lambda i: (0,) * len(shape))

    widths = (HA_WIDTH, HB_WIDTH, HC_WIDTH)
    return pl.pallas_call(
        _post_body,
        grid=(N_ALL // tm,),
        in_specs=[prows(wd) for wd in widths] + [full((tm, wd)) for wd in widths]
                 + [rows(d), full((d, d)), full((1, d)), full((d, N_EXPERTS)), full((1, N_EXPERTS))],
        out_specs=[rows(d), rows(PACK_W), rows(ROUTE_LANES), rows(ROUTE_LANES)],
        out_shape=[jax.ShapeDtypeStruct((N_ALL, d), F32), jax.ShapeDtypeStruct((N_ALL, PACK_W), jnp.uint32),
                   jax.ShapeDtypeStruct((N_ALL, ROUTE_LANES), jnp.int32),
                   jax.ShapeDtypeStruct((N_ALL, ROUTE_LANES), F32)],
        compiler_params=_cparams(("parallel",)),
        name="post_mixer",
    )(*prompt_mix, *sample_mix, x, w_out, _row(nf), rw, _row(rb))


def _unpack_rows(words):
    lo = lax.bitcast_convert_type(words << 16, F32).astype(BF16)
    hi = lax.bitcast_convert_type(words & jnp.uint32(0xFFFF0000), F32).astype(BF16)
    return lo, hi


def _expert_tiles(ts_ref, nt_ref, src_ref, dst_ref, tn, in_buf, out_buf, sem_in, sem_out, compute):
    n, e = pl.program_id(0), pl.program_id(1)
    step = n * pl.num_programs(1) + e
    last_step = pl.num_programs(0) * pl.num_programs(1) - 1
    t0, nt = ts_ref[e], nt_ref[e]
    tm = MOE_TM
    nbuf = MOE_LOOKAHEAD + 1

    def in_copy(tile, slot):
        return pltpu.make_async_copy(src_ref.at[pl.ds(tile * tm, tm)], in_buf.at[slot], sem_in.at[slot])

    def out_copy(tile, slot):
        return pltpu.make_async_copy(out_buf.at[slot], dst_ref.at[pl.ds(tile * tm, tm), pl.ds(n * tn, tn)],
                                     sem_out.at[slot])

    def request_first_tiles(first_tile, count):
        for j in range(MOE_LOOKAHEAD):
            @pl.when(j < count)
            def _():
                in_copy(first_tile + j, j).start(priority=TILE_DMA_PRIORITY)

    @pl.when(step == 0)
    def _():
        request_first_tiles(t0, nt)

    def tile_body(i, carry):
        slot = i % nbuf
        in_copy(t0 + i, slot).wait()

        @pl.when(i + MOE_LOOKAHEAD < nt)
        def _():
            in_copy(t0 + i + MOE_LOOKAHEAD, (i + MOE_LOOKAHEAD) % nbuf).start(priority=TILE_DMA_PRIORITY)

        res = compute(in_buf[slot])
        oslot = i % 2

        @pl.when(i >= 2)
        def _():
            out_copy(t0 + i - 2, oslot).wait()

        out_buf[oslot] = res
        out_copy(t0 + i, oslot).start(priority=TILE_DMA_PRIORITY)
        return carry

    lax.fori_loop(0, nt, tile_body, 0)

    @pl.when(nt >= 2)
    def _():
        out_copy(t0 + nt - 2, nt % 2).wait()

    @pl.when(nt >= 1)
    def _():
        out_copy(t0 + nt - 1, (nt - 1) % 2).wait()

    e_next = jnp.where(e + 1 < pl.num_programs(1), e + 1, 0)

    @pl.when(step < last_step)
    def _():
        request_first_tiles(ts_ref[e_next], nt_ref[e_next])

    @pl.when(e == pl.num_programs(1) - 1)
    def _():
        out_buf[0] = jnp.zeros(out_buf.shape[1:], out_buf.dtype)

        def fill(tile, carry):
            out_copy(tile, 0).start()
            out_copy(tile, 0).wait()
            return carry

        lax.fori_loop(t0 + nt, MOE_TILES, fill, 0)


def _gmm1_body(ts_ref, nt_ref, xs_ref, wg_ref, wl_ref, bg_ref, bl_ref, act_ref,
               in_buf, out_buf, w_scr, sem_in, sem_out):
    tn = MOE_TN1

    @pl.when(nt_ref[pl.program_id(1)] > 0)
    def _():
        w_scr[:, :tn] = wg_ref[0, 0].astype(BF16)
        w_scr[:, tn:] = wl_ref[0, 0].astype(BF16)

    def compute(words):
        lo, hi = _unpack_rows(words)
        u = (jnp.dot(lo, w_scr[:PACK_W, :], preferred_element_type=F32)
             + jnp.dot(hi, w_scr[PACK_W:, :], preferred_element_type=F32))
        glu = jnp.minimum(u[:, :tn] + bg_ref[0, 0], SWIGLU_LIMIT)
        lin = jnp.clip(u[:, tn:] + bl_ref[0, 0], -SWIGLU_LIMIT, SWIGLU_LIMIT)
        return (glu * _sigmoid(SWIGLU_ALPHA * glu) * (lin + 1.0)).astype(BF16)

    _expert_tiles(ts_ref, nt_ref, xs_ref, act_ref, tn, in_buf, out_buf, sem_in, sem_out, compute)


def _gmm2_body(ts_ref, nt_ref, act_ref, w_ref, b_ref, yo_ref, in_buf, out_buf, w_scr, sem_in, sem_out):
    @pl.when(nt_ref[pl.program_id(1)] > 0)
    def _():
        w_scr[...] = w_ref[0, 0].astype(BF16)

    def compute(a):
        return jnp.dot(a, w_scr[...], preferred_element_type=F32) + b_ref[0, 0]

    _expert_tiles(ts_ref, nt_ref, act_ref, yo_ref, MOE_TN2, in_buf, out_buf, sem_in, sem_out, compute)


def _gmm_scratch(in_shape, in_dtype, tn, out_dtype):
    nbuf = MOE_LOOKAHEAD + 1
    return [pltpu.VMEM((nbuf,) + in_shape, in_dtype), pltpu.VMEM((2, MOE_TM, tn), out_dtype),
            pltpu.SemaphoreType.DMA((nbuf,)), pltpu.SemaphoreType.DMA((2,))]


def moe_gmm1(layer, xs, w1, b1, tile_start, tile_count):
    tn = MOE_TN1
    nb = D_FF // tn
    d = D_MODEL
    any_spec = pl.BlockSpec(memory_space=pl.ANY)
    in_buf, out_buf, sem_in, sem_out = _gmm_scratch((MOE_TM, PACK_W), jnp.uint32, tn, BF16)
    grid_spec = pltpu.PrefetchScalarGridSpec(
        num_scalar_prefetch=2,
        grid=(nb, N_EXPERTS),
        in_specs=[
            any_spec,
            pl.BlockSpec((1, 1, d, tn), lambda n, e, ts, nt: (layer, e, 0, n)),
            pl.BlockSpec((1, 1, d, tn), lambda n, e, ts, nt: (layer, e, 0, n + nb)),
            pl.BlockSpec((1, 1, 1, tn), lambda n, e, ts, nt: (layer, e, 0, n)),
            pl.BlockSpec((1, 1, 1, tn), lambda n, e, ts, nt: (layer, e, 0, n + nb)),
        ],
        out_specs=any_spec,
        scratch_shapes=[in_buf, out_buf, pltpu.VMEM((d, 2 * tn), BF16), sem_in, sem_out],
    )
    b1r = b1.reshape(DEPTH, N_EXPERTS, 1, 2 * D_FF)
    return pl.pallas_call(
        _gmm1_body,
        grid_spec=grid_spec,
        out_shape=jax.ShapeDtypeStruct((MOE_ROWS, D_FF), BF16),
        compiler_params=_cparams(("arbitrary", "arbitrary")),
        name="moe_gmm1",
    )(tile_start, tile_count, xs, w1, w1, b1r, b1r)


def moe_gmm2(layer, act, w2, b2, tile_start, tile_count):
    tn = MOE_TN2
    nb = D_MODEL // tn
    any_spec = pl.BlockSpec(memory_space=pl.ANY)
    in_buf, out_buf, sem_in, sem_out = _gmm_scratch((MOE_TM, D_FF), BF16, tn, F32)
    grid_spec = pltpu.PrefetchScalarGridSpec(
        num_scalar_prefetch=2,
        grid=(nb, N_EXPERTS),
        in_specs=[
            any_spec,
            pl.BlockSpec((1, 1, D_FF, tn), lambda n, e, ts, nt: (layer, e, 0, n)),
            pl.BlockSpec((1, 1, 1, tn), lambda n, e, ts, nt: (layer, e, 0, n)),
        ],
        out_specs=any_spec,
        scratch_shapes=[in_buf, out_buf, pltpu.VMEM((D_FF, tn), BF16), sem_in, sem_out],
    )
    return pl.pallas_call(
        _gmm2_body,
        grid_spec=grid_spec,
        out_shape=jax.ShapeDtypeStruct((MOE_ROWS, D_MODEL), F32),
        compiler_params=_cparams(("arbitrary", "arbitrary")),
        name="moe_gmm2",
    )(tile_start, tile_count, act, w2, b2.reshape(DEPTH, N_EXPERTS, 1, D_MODEL))


DISP_CHUNK = 256
N_SLOTS = N_ALL * TOP_K


def _dispatch_body(pos_ref, hp_ref, xs_in_ref, xs_ref, sem):
    del xs_in_ref
    base = pl.program_id(0) * (DISP_CHUNK * TOP_K)

    def row_copy(i, k):
        return pltpu.make_async_copy(hp_ref.at[pl.ds(i, 1)], xs_ref.at[pl.ds(pos_ref[base + i * TOP_K + k], 1)], sem)

    def all_rows(start):
        def body(i, carry):
            for k in range(TOP_K):
                cp = row_copy(i, k)
                cp.start(priority=k % 2) if start else cp.wait()
            return carry
        lax.fori_loop(0, DISP_CHUNK, body, 0, unroll=4)

    all_rows(True)
    all_rows(False)


def moe_dispatch(pos, hp):
    any_spec = pl.BlockSpec(memory_space=pl.ANY)
    grid_spec = pltpu.PrefetchScalarGridSpec(
        num_scalar_prefetch=1,
        grid=(N_ALL // DISP_CHUNK,),
        in_specs=[pl.BlockSpec((DISP_CHUNK, PACK_W), lambda i, pos: (i, 0)), any_spec],
        out_specs=any_spec,
        scratch_shapes=[pltpu.SemaphoreType.DMA(())],
    )
    xs0 = jnp.zeros((MOE_ROWS, PACK_W), jnp.uint32)
    return pl.pallas_call(
        _dispatch_body,
        grid_spec=grid_spec,
        out_shape=jax.ShapeDtypeStruct((MOE_ROWS, PACK_W), jnp.uint32),
        input_output_aliases={2: 0},
        compiler_params=_cparams(("arbitrary",)),
        name="moe_dispatch",
    )(pos, hp, xs0)


COMB_TB = 128


def _combine_body(final, pos_ref, yo_ref, x1_ref, tg_ref, nf_ref, o_ref, buf, sem):
    s = pl.program_id(0)
    ns = pl.num_programs(0)

    def row_copy(step, i, k, slot):
        j = (step * COMB_TB + i) * TOP_K + k
        return pltpu.make_async_copy(yo_ref.at[pl.ds(pos_ref[j], 1)], buf.at[slot, k, pl.ds(i, 1)], sem.at[slot])

    def start_tile(step, slot):
        def body(i, carry):
            for k in range(TOP_K):
                row_copy(step, i, k, slot).start(priority=k % 2)
            return carry
        lax.fori_loop(0, COMB_TB, body, 0, unroll=4)

    def wait_tile(slot):
        for k in range(TOP_K):
            pltpu.make_async_copy(yo_ref.at[pl.ds(0, COMB_TB)], buf.at[slot, k], sem.at[slot]).wait()

    @pl.when(s == 0)
    def _():
        start_tile(0, 0)

    @pl.when(s + 1 < ns)
    def _():
        start_tile(s + 1, (s + 1) % 2)

    slot = s % 2
    wait_tile(slot)
    acc = x1_ref[...]
    for k in range(TOP_K):
        acc = acc + tg_ref[:, k:k + 1] * buf[slot, k]
    o_ref[...] = _rms(acc, nf_ref[...]) if final else acc


def moe_combine(pos, yo, x1, tg, nf, final):
    tb = COMB_TB
    grid_spec = pltpu.PrefetchScalarGridSpec(
        num_scalar_prefetch=1,
        grid=(N_ALL // tb,),
        in_specs=[pl.BlockSpec(memory_space=pl.ANY),
                  pl.BlockSpec((tb, D_MODEL), lambda i, pos: (i, 0)),
                  pl.BlockSpec((tb, ROUTE_LANES), lambda i, pos: (i, 0)),
                  pl.BlockSpec((1, D_MODEL), lambda i, pos: (0, 0))],
        out_specs=pl.BlockSpec((tb, D_MODEL), lambda i, pos: (i, 0)),
        scratch_shapes=[pltpu.VMEM((2, TOP_K, tb, D_MODEL), F32), pltpu.SemaphoreType.DMA((2,))],
    )
    return pl.pallas_call(
        functools.partial(_combine_body, final),
        grid_spec=grid_spec,
        out_shape=jax.ShapeDtypeStruct((N_ALL, D_MODEL), F32),
        compiler_params=_cparams(("arbitrary",)),
        name="moe_combine",
    )(pos, yo, x1, tg, _row(nf))


def moe_layout(ti):
    flat_e = ti.reshape(-1)
    oh = (flat_e[:, None] == jnp.arange(N_EXPERTS, dtype=jnp.int32)[None, :]).astype(jnp.int32)
    csum = jnp.cumsum(oh, axis=0)
    counts = csum[-1]
    rank = jnp.sum((csum - oh) * oh, axis=1)
    tile_count = (counts + MOE_TM - 1) // MOE_TM
    tile_start = jnp.cumsum(tile_count) - tile_count
    pos = (tile_start * MOE_TM)[flat_e] + rank
    return pos.astype(jnp.int32), tile_start.astype(jnp.int32), tile_count.astype(jnp.int32)


def _block_diag(w):
    eye = jnp.eye(HB_BLOCKS, dtype=w.dtype)
    return jnp.einsum("hij,hg->higj", w, eye).reshape(HB_WIDTH, HB_WIDTH)


def kernel(x_prompt, x_sample, mem_prompt, state_hgrn, state_lru, state_conv, cache_mem_k, cache_mem_v,
           norm_mix, w_in, hgrn_lb, hgrn_onorm, conv_w, conv_b, lru_wa, lru_ba, lru_wx, lru_bx, lru_L,
           norm_lru, norm_xattn, norm_mem, w_mem_k, w_mem_v, w_out, norm_ffn, router_w, router_b,
           moe_w1, moe_b1, moe_w2, moe_b2, norm_final):
    x = jnp.concatenate([x_prompt.reshape(N_P, D_MODEL), x_sample.reshape(N_S, D_MODEL)], axis=0)
    mem = mem_prompt.reshape(BATCH * N_MEM, D_MODEL)
    cache_k = cache_mem_k.reshape(DEPTH, DEC_BATCH, N_MEM * HC_HEADS, HC_DH)
    cache_v = cache_mem_v.reshape(DEPTH, DEC_BATCH, N_MEM * HC_HEADS, HC_DH)
    conv_s = state_conv.reshape(DEPTH, DEC_BATCH, (CONV_W - 1) * HB_WIDTH)
    hgrn_s_out = jnp.zeros(state_hgrn.shape, F32)

    p_hgrn, p_lru, p_conv, p_mk, p_mv, s_lru, s_conv = [], [], [], [], [], [], []
    for l in range(DEPTH):
        w_mem = jnp.concatenate([w_mem_k[l], w_mem_v[l]], axis=1).astype(BF16)
        memkv = norm_matmul(mem, norm_mem[l], w_mem, tm=BATCH * N_MEM, tn=512)
        p_mk.append(memkv[:, :HC_WIDTH].reshape(BATCH, N_MEM, HC_HEADS, HC_DH))
        p_mv.append(memkv[:, HC_WIDTH:].reshape(BATCH, N_MEM, HC_HEADS, HC_DH))

        u = norm_matmul(x, norm_mix[l], w_in[l].astype(BF16), tm=N_ALL // 8, tn=512)

        oa_p, hg_p = hgrn_prompt(l, u, hgrn_lb, hgrn_onorm[l])
        oa_s, hgrn_s_out = hgrn_sample(l, u, hgrn_lb, hgrn_onorm[l], state_hgrn, hgrn_s_out)

        wa = _block_diag(lru_wa[l]).astype(BF16)
        wx = _block_diag(lru_wx[l]).astype(BF16)
        lru_args = (conv_w[l], conv_b[l], wa, wx, lru_ba[l], lru_bx[l], lru_L[l], norm_lru[l])
        ob_p, hl_p, cs_p = lru_prompt(u, *lru_args)
        ob_s, hl_s, cs_s = lru_sample(l, u, state_lru, conv_s, *lru_args)

        oc_p = xattn_prompt(u, memkv, norm_xattn[l])
        oc_s = xattn_sample(l, u, cache_k, cache_v, norm_xattn[l])

        x1, hp, ti, tg = post_mixer((oa_p, ob_p, oc_p), (oa_s, ob_s, oc_s), x, w_out[l].astype(BF16),
                                    norm_ffn[l], router_w[l], router_b[l])
        pos, tile_start, tile_count = moe_layout(ti[:, :TOP_K])
        xs = moe_dispatch(pos, hp)
        act = moe_gmm1(l, xs, moe_w1, moe_b1, tile_start, tile_count)
        yo = moe_gmm2(l, act, moe_w2, moe_b2, tile_start, tile_count)
        x = moe_combine(pos, yo, x1, tg, norm_final, final=(l == DEPTH - 1))

        p_hgrn.append(hg_p)
        p_lru.append(hl_p.reshape(BATCH, HB_WIDTH))
        p_conv.append(cs_p)
        s_lru.append(hl_s)
        s_conv.append(cs_s.reshape(DEC_BATCH, CONV_W - 1, HB_WIDTH))

    y_prompt = x[:N_P].reshape(BATCH, SEQ, D_MODEL)
    y_sample = x[N_P:].reshape(DEC_BATCH, DEC_SEQ, D_MODEL)
    return (y_prompt, y_sample, jnp.stack(p_hgrn), jnp.stack(p_lru), jnp.stack(p_conv),
            jnp.stack(p_mk), jnp.stack(p_mv), hgrn_s_out, jnp.stack(s_lru), jnp.stack(s_conv))
```
